```python
import jax, jax.numpy as jnp
from jax import lax
import numpy as np

D_MODEL = 1024
BATCH = 16
SEQ = 2048
DEPTH = 1

SB_HEADS = 8
SB_HEAD_DIM = 64
MLA_HEADS = 8
MLA_NOPE_DIM = 64
MLA_ROPE_DIM = 32
MLA_V_DIM = 64
Q_LORA_RANK = 384
KV_LORA_RANK = 256
D_FF = 2816
CONV_WIDTH = 3
BLOCK_Q = 128
ROPE_BASE = 10000.0
EPS = 1e-6

SB_WIDTH = SB_HEADS * SB_HEAD_DIM
MLA_WIDTH = MLA_HEADS * MLA_V_DIM
MIX_WIDTH = SB_WIDTH + MLA_WIDTH
IN_COLS = 3 * SB_WIDTH + Q_LORA_RANK + KV_LORA_RANK + MLA_ROPE_DIM
MLA_QK_DIM = MLA_NOPE_DIM + MLA_ROPE_DIM

kernel_name = "hymba_style_stickbreak_mla_convffn"


def rmsnorm(x, g):
    xf = x.astype(jnp.float32)
    y = xf * lax.rsqrt(jnp.mean(xf * xf, axis=-1, keepdims=True) + EPS)
    return (y * g.astype(jnp.float32)).astype(x.dtype)


def rope_tables(positions, dim):
    half = dim // 2
    inv_freq = 1.0 / (ROPE_BASE ** (jnp.arange(half, dtype=jnp.float32) * (2.0 / dim)))
    ang = positions.astype(jnp.float32)[..., None] * inv_freq
    return jnp.cos(ang), jnp.sin(ang)


def apply_rope(x, cos, sin):
    half = x.shape[-1] // 2
    xf = x.astype(jnp.float32)
    x1, x2 = xf[..., :half], xf[..., half:]
    out = jnp.concatenate([x1 * cos - x2 * sin, x2 * cos + x1 * sin], axis=-1)
    return out.astype(x.dtype)


def stick_breaking_attention(q, k, v):
    S, Dh = q.shape[1], q.shape[-1]
    scale = Dh ** -0.5
    outs = []
    for i in range(S // BLOCK_Q):
        q0 = i * BLOCK_Q
        kv_len = q0 + BLOCK_Q
        qb = q[:, q0:kv_len]
        kb = k[:, :kv_len]
        vb = v[:, :kv_len]
        z = jnp.einsum('bqhd,bkhd->bhqk', qb, kb, preferred_element_type=jnp.float32) * scale
        t_idx = q0 + jnp.arange(BLOCK_Q)[:, None]
        s_idx = jnp.arange(kv_len)[None, :]
        visible = s_idx < t_idx
        log_beta = jax.nn.log_sigmoid(z)
        log_keep = jnp.where(visible, jax.nn.log_sigmoid(-z), 0.0)
        tail = lax.cumsum(log_keep, axis=3, reverse=True) - log_keep
        a = jnp.where(visible, jnp.exp(log_beta + tail), 0.0)
        outs.append(jnp.einsum('bhqk,bkhd->bqhd', a.astype(v.dtype), vb))
    return jnp.concatenate(outs, axis=1)


def mla_attention(q_nope, q_rope, k_nope, k_rope, v):
    S = q_nope.shape[1]
    scale = MLA_QK_DIM ** -0.5
    outs = []
    for i in range(S // BLOCK_Q):
        q0 = i * BLOCK_Q
        kv_len = q0 + BLOCK_Q
        s = (jnp.einsum('bqhd,bkhd->bhqk', q_nope[:, q0:kv_len], k_nope[:, :kv_len],
                        preferred_element_type=jnp.float32)
             + jnp.einsum('bqhr,bkr->bhqk', q_rope[:, q0:kv_len], k_rope[:, :kv_len],
                          preferred_element_type=jnp.float32)) * scale
        t_idx = q0 + jnp.arange(BLOCK_Q)[:, None]
        s_idx = jnp.arange(kv_len)[None, :]
        s = jnp.where(s_idx <= t_idx, s, -jnp.inf)
        p = jax.nn.softmax(s, axis=-1)
        outs.append(jnp.einsum('bhqk,bkhd->bqhd', p.astype(v.dtype), v[:, :kv_len]))
    return jnp.concatenate(outs, axis=1)


def causal_depthwise_conv(h, w, b):
    C = h.shape[-1]
    y = lax.conv_general_dilated(
        h, w[:, None, :].astype(h.dtype), window_strides=(1,),
        padding=[(CONV_WIDTH - 1, 0)],
        dimension_numbers=('NWC', 'WIO', 'NWC'),
        feature_group_count=C)
    return y + b.astype(h.dtype)


def setup_inputs(seed: int = 0) -> dict:
    key = jax.random.key(seed)
    ks = jax.random.split(key, 20)
    f32 = jnp.float32

    def nrm(k, shape, fan_in):
        return jax.random.normal(k, shape, f32) * (fan_in ** -0.5)

    def gain(k, shape):
        return 1.0 + 0.02 * jax.random.normal(k, shape, f32)

    x = jax.random.normal(ks[0], (BATCH, SEQ, D_MODEL), f32)
    offset = jax.random.randint(ks[1], (BATCH, 1), 0, 1024, dtype=jnp.int32)
    positions = (jnp.arange(SEQ, dtype=jnp.int32)[None, :] + offset).astype(jnp.int32)
    return {
        "x": x,
        "positions": positions,
        "g_mix": gain(ks[2], (DEPTH, D_MODEL)),
        "w_in": nrm(ks[3], (DEPTH, D_MODEL, IN_COLS), D_MODEL),
        "g_cq": gain(ks[4], (DEPTH, Q_LORA_RANK)),
        "w_uq": nrm(ks[5], (DEPTH, Q_LORA_RANK, MLA_HEADS * MLA_QK_DIM), Q_LORA_RANK),
        "g_ckv": gain(ks[6], (DEPTH, KV_LORA_RANK)),
        "w_ukv": nrm(ks[7], (DEPTH, KV_LORA_RANK, MLA_HEADS * (MLA_NOPE_DIM + MLA_V_DIM)), KV_LORA_RANK),
        "g_sb_out": gain(ks[8], (DEPTH, SB_WIDTH)),
        "g_mla_out": gain(ks[9], (DEPTH, MLA_WIDTH)),
        "w_out": nrm(ks[10], (DEPTH, MIX_WIDTH, D_MODEL), MIX_WIDTH),
        "g_ffn": gain(ks[11], (DEPTH, D_MODEL)),
        "w_up": nrm(ks[12], (DEPTH, D_MODEL, 2 * D_FF), D_MODEL),
        "conv_w": nrm(ks[13], (DEPTH, CONV_WIDTH, 2 * D_FF), CONV_WIDTH),
        "conv_b": 0.01 * jax.random.normal(ks[14], (DEPTH, 2 * D_FF), f32),
        "w_down": nrm(ks[15], (DEPTH, D_FF, D_MODEL), D_FF),
        "g_final": gain(ks[16], (D_MODEL,)),
    }


def reference(x, positions, g_mix, w_in, g_cq, w_uq, g_ckv, w_ukv, g_sb_out, g_mla_out,
              w_out, g_ffn, w_up, conv_w, conv_b, w_down, g_final):
    B, S, _ = x.shape
    cos, sin = rope_tables(positions, MLA_ROPE_DIM)
    cos_h, sin_h = cos[:, :, None, :], sin[:, :, None, :]
    split_at = np.cumsum([SB_WIDTH, SB_WIDTH, SB_WIDTH, Q_LORA_RANK, KV_LORA_RANK])

    for l in range(DEPTH):
        h = rmsnorm(x, g_mix[l])
        p = h @ w_in[l]
        q_sb, k_sb, v_sb, c_q, c_kv, k_rope = jnp.split(p, split_at, axis=-1)

        o_sb = stick_breaking_attention(
            q_sb.reshape(B, S, SB_HEADS, SB_HEAD_DIM),
            k_sb.reshape(B, S, SB_HEADS, SB_HEAD_DIM),
            v_sb.reshape(B, S, SB_HEADS, SB_HEAD_DIM)).reshape(B, S, SB_WIDTH)

        q = (rmsnorm(c_q, g_cq[l]) @ w_uq[l]).reshape(B, S, MLA_HEADS, MLA_QK_DIM)
        q_nope, q_rope = q[..., :MLA_NOPE_DIM], q[..., MLA_NOPE_DIM:]
        q_rope = apply_rope(q_rope, cos_h, sin_h)
        kv = (rmsnorm(c_kv, g_ckv[l]) @ w_ukv[l]).reshape(B, S, MLA_HEADS, MLA_NOPE_DIM + MLA_V_DIM)
        k_nope, v_mla = kv[..., :MLA_NOPE_DIM], kv[..., MLA_NOPE_DIM:]
        k_rope = apply_rope(k_rope, cos, sin)
        o_mla = mla_attention(q_nope, q_rope, k_nope, k_rope, v_mla).reshape(B, S, MLA_WIDTH)

        o = jnp.concatenate([rmsnorm(o_sb, g_sb_out[l]), rmsnorm(o_mla, g_mla_out[l])], axis=-1)
        x = x + o @ w_out[l]

        u = rmsnorm(x, g_ffn[l]) @ w_up[l]
        u = causal_depthwise_conv(u, conv_w[l], conv_b[l])
        gate, val = u[..., :D_FF], u[..., D_FF:]
        x = x + (jax.nn.silu(gate) * val) @ w_down[l]

    return rmsnorm(x, g_final)
```

```python
import functools

import numpy as np
import jax
import jax.numpy as jnp
from jax import lax
from jax.experimental import pallas as pl
from jax.experimental.pallas import tpu as pltpu

SB_HEADS = 8
SB_HEAD_DIM = 64
MLA_HEADS = 8
MLA_NOPE_DIM = 64
MLA_ROPE_DIM = 32
MLA_V_DIM = 64
Q_LORA_RANK = 384
KV_LORA_RANK = 256
D_FF = 2816
CONV_WIDTH = 3
ROPE_BASE = 10000.0
EPS = 1e-6

SB_WIDTH = SB_HEADS * SB_HEAD_DIM
MLA_WIDTH = MLA_HEADS * MLA_V_DIM
MLA_QK_DIM = MLA_NOPE_DIM + MLA_ROPE_DIM

LANES = 128
BF16_SUBLANES = 16
VMEM_LIMIT_BYTES = 56 * 1024 * 1024

MLA_HEAD_PAD = LANES
ROPE_LANE0 = MLA_NOPE_DIM
ROPE_HALF = MLA_ROPE_DIM // 2

BF16 = jnp.bfloat16
F32 = jnp.float32


def _rms(x, g):
    return x * lax.rsqrt(jnp.mean(x * x, axis=-1, keepdims=True) + EPS) * g


def _dot(a, b):
    return jnp.dot(a, b, preferred_element_type=F32)


def _dot_nt(a, b):
    return lax.dot_general(a, b, (((1,), (1,)), ((), ())), preferred_element_type=F32)


def _in_proj_body(x_ref, pos_ref, g_mix_ref, w_in_ref, g_cq_ref, w_uq_ref, g_ckv_ref, w_uk_ref,
                  w_uv_ref, invf_ref, sgn_ref,
                  qsb_ref, ksb_ref, vsb_ref, qm_ref, km_ref, vm_ref):
    h = _rms(x_ref[...], g_mix_ref[...]).astype(BF16)

    c0 = 0
    qsb_ref[...] = (_dot(h, w_in_ref[:, c0:c0 + SB_WIDTH]) * (SB_HEAD_DIM ** -0.5)).astype(BF16)
    c0 += SB_WIDTH
    ksb_ref[...] = _dot(h, w_in_ref[:, c0:c0 + SB_WIDTH]).astype(BF16)
    c0 += SB_WIDTH
    vsb_ref[...] = _dot(h, w_in_ref[:, c0:c0 + SB_WIDTH]).astype(BF16)
    c0 += SB_WIDTH
    c_q = _dot(h, w_in_ref[:, c0:c0 + Q_LORA_RANK])
    c0 += Q_LORA_RANK
    c_kv = _dot(h, w_in_ref[:, c0:c0 + KV_LORA_RANK])
    c0 += KV_LORA_RANK
    k_rope = _dot(h, w_in_ref[:, c0:c0 + LANES])

    ang = pos_ref[...] * invf_ref[...]
    cos_t = jnp.cos(ang)
    sin_t = jnp.sin(ang) * sgn_ref[...]
    lane = lax.broadcasted_iota(jnp.int32, (1, LANES), 1)
    first_half = lane < ROPE_LANE0 + ROPE_HALF

    def rope(v):
        partner = jnp.where(first_half, pltpu.roll(v, LANES - ROPE_HALF, 1), pltpu.roll(v, ROPE_HALF, 1))
        return v * cos_t + partner * sin_t

    cqn = _rms(c_q, g_cq_ref[...]).astype(BF16)
    q = _dot(cqn, w_uq_ref[...])
    ckvn = _rms(c_kv, g_ckv_ref[...]).astype(BF16)
    k_nope = _dot(ckvn, w_uk_ref[...])
    vm_ref[...] = _dot(ckvn, w_uv_ref[...]).astype(BF16)
    k_rope = rope(k_rope)
    for hd in range(MLA_HEADS):
        sl = slice(hd * MLA_HEAD_PAD, (hd + 1) * MLA_HEAD_PAD)
        qm_ref[:, sl] = (rope(q[:, sl]) * (MLA_QK_DIM ** -0.5)).astype(BF16)
        km_ref[:, sl] = (k_nope[:, sl] + k_rope).astype(BF16)


def _in_proj(x2d, pos, g_mix, w_in_p, g_cq, w_uq_p, g_ckv, w_uk_p, w_uv, invf, sgn, *, tm):
    t, d = x2d.shape
    row = lambda w: pl.BlockSpec((tm, w), lambda i: (i, 0))
    full = lambda a: pl.BlockSpec(a.shape, lambda i: (0, 0))
    outs = [(SB_WIDTH, BF16)] * 3 + [(MLA_HEADS * MLA_HEAD_PAD, BF16)] * 2 + [(MLA_WIDTH, BF16)]
    return pl.pallas_call(
        _in_proj_body,
        grid=(t // tm,),
        in_specs=[row(d), row(1), full(g_mix), full(w_in_p), full(g_cq), full(w_uq_p), full(g_ckv),
                  full(w_uk_p), full(w_uv), full(invf), full(sgn)],
        out_specs=[row(w) for w, _ in outs],
        out_shape=[jax.ShapeDtypeStruct((t, w), dt) for w, dt in outs],
        compiler_params=pltpu.CompilerParams(dimension_semantics=("arbitrary",),
                                             vmem_limit_bytes=VMEM_LIMIT_BYTES),
        name="in_proj",
    )(x2d, pos, g_mix, w_in_p, g_cq, w_uq_p, g_ckv, w_uk_p, w_uv, invf, sgn)


def _sb_body(q_ref, k_ref, v_ref, o_ref, *, tq, tk):
    i = pl.program_id(2)
    lane = lax.broadcasted_iota(jnp.int32, (1, LANES), 1)
    lo = lane < SB_HEAD_DIM
    q = q_ref[0]
    zero = jnp.zeros_like(q)
    q_heads = (jnp.where(lo, q, zero), jnp.where(lo, zero, q))

    jj = lax.broadcasted_iota(jnp.int32, (tk, tk + LANES), 0)
    ss = lax.broadcasted_iota(jnp.int32, (tk, tk + LANES), 1)
    suffix = jnp.where((jj > ss) | (ss >= tk), 1.0, 0.0).astype(BF16)

    row0 = i * tq
    j_diag = row0 // tk
    r_idx = lax.broadcasted_iota(jnp.int32, (tq, tk), 0) + row0
    c_idx = lax.broadcasted_iota(jnp.int32, (tq, tk), 1)

    def tile(j, carry, masked):
        acc, tails = carry
        k0 = pl.multiple_of(j * tk, tk)
        kt = k_ref[0, pl.ds(k0, tk), :]
        vt = v_ref[0, pl.ds(k0, tk), :]
        vzero = jnp.zeros_like(vt)
        v_heads = (jnp.where(lo, vt, vzero), jnp.where(lo, vzero, vt))
        new_tails = []
        for hd in range(2):
            z = _dot_nt(q_heads[hd], kt)
            log_keep = -(jnp.maximum(z, 0.0) + jnp.log1p(jnp.exp(-jnp.abs(z))))
            log_beta = log_keep + z
            if masked:
                visible = c_idx + k0 < r_idx
                log_keep = jnp.where(visible, log_keep, 0.0)
            sums = _dot(log_keep.astype(BF16), suffix)
            tail = sums[:, :tk] + jnp.tile(tails[hd], (1, tk // LANES))
            a = jnp.exp(log_beta + tail)
            if masked:
                a = jnp.where(visible, a, 0.0)
            acc = acc + _dot(a.astype(BF16), v_heads[hd])
            new_tails.append(tails[hd] + sums[:, tk:])
        return acc, tuple(new_tails)

    zeros = jnp.zeros((tq, LANES), F32)
    carry = tile(j_diag, (zeros, (zeros, zeros)), True)
    carry = lax.fori_loop(0, j_diag, lambda s, c: tile(j_diag - 1 - s, c, False), carry)
    o_ref[0] = carry[0]


def _sb_attention(q, k, v, *, tq, tk):
    b, s, w = q.shape
    pairs = w // LANES
    return pl.pallas_call(
        functools.partial(_sb_body, tq=tq, tk=tk),
        grid=(b, pairs, s // tq),
        in_specs=[pl.BlockSpec((1, tq, LANES), lambda bi, hp, i: (bi, i, hp)),
                  pl.BlockSpec((1, s, LANES), lambda bi, hp, i: (bi, 0, hp)),
                  pl.BlockSpec((1, s, LANES), lambda bi, hp, i: (bi, 0, hp))],
        out_specs=pl.BlockSpec((1, tq, LANES), lambda bi, hp, i: (bi, i, hp)),
        out_shape=jax.ShapeDtypeStruct((b, s, w), F32),
        compiler_params=pltpu.CompilerParams(dimension_semantics=("arbitrary",) * 3,
                                             vmem_limit_bytes=VMEM_LIMIT_BYTES),
        name="sb_attention",
    )(q, k, v)


def _mla_body(q_ref, k_ref, v_ref, o_ref, *, tq, tk):
    i = pl.program_id(2)
    lane = lax.broadcasted_iota(jnp.int32, (1, LANES), 1)
    lo = lane < MLA_V_DIM
    row0 = i * tq
    j_diag = row0 // tk
    r_idx = lax.broadcasted_iota(jnp.int32, (tq, tk), 0) + row0
    c_idx = lax.broadcasted_iota(jnp.int32, (tq, tk), 1)

    outs = []
    for hd in range(2):
        q = q_ref[0, :, hd * MLA_HEAD_PAD:(hd + 1) * MLA_HEAD_PAD]

        def tile(j, carry, masked, hd=hd, q=q):
            m, acc = carry
            k0 = pl.multiple_of(j * tk, tk)
            kt = k_ref[0, pl.ds(k0, tk), hd * MLA_HEAD_PAD:(hd + 1) * MLA_HEAD_PAD]
            vt = v_ref[0, pl.ds(k0, tk), :]
            one = jnp.ones_like(vt)
            v_ext = jnp.where(lo, vt, one) if hd == 0 else jnp.where(lo, one, vt)
            s = _dot_nt(q, kt)
            if masked:
                s = jnp.where(c_idx + k0 <= r_idx, s, -jnp.inf)
            m_new = jnp.maximum(m, jnp.max(s, axis=-1, keepdims=True))
            alpha = jnp.exp(m - m_new)
            p = jnp.exp(s - jnp.tile(m_new, (1, tk // LANES)))
            acc = alpha * acc + _dot(p.astype(BF16), v_ext)
            return m_new, acc

        carry = (jnp.full((tq, LANES), -jnp.inf, F32), jnp.zeros((tq, LANES), F32))
        carry = tile(j_diag, carry, True)
        carry = lax.fori_loop(0, j_diag, lambda st, c: tile(st, c, False), carry)
        acc = carry[1]
        outs.append(acc / pltpu.roll(acc, MLA_V_DIM, 1))
    o_ref[0] = jnp.where(lo, outs[0], outs[1])


def _mla_attention(q, k, v, *, tq, tk):
    b, s, _ = q.shape
    w = v.shape[-1]
    pairs = w // LANES
    return pl.pallas_call(
        functools.partial(_mla_body, tq=tq, tk=tk),
        grid=(b, pairs, s // tq),
        in_specs=[pl.BlockSpec((1, tq, 2 * MLA_HEAD_PAD), lambda bi, hp, i: (bi, i, hp)),
                  pl.BlockSpec((1, s, 2 * MLA_HEAD_PAD), lambda bi, hp, i: (bi, 0, hp)),
                  pl.BlockSpec((1, s, LANES), lambda bi, hp, i: (bi, 0, hp))],
        out_specs=pl.BlockSpec((1, tq, LANES), lambda bi, hp, i: (bi, i, hp)),
        out_shape=jax.ShapeDtypeStruct((b, s, w), F32),
        compiler_params=pltpu.CompilerParams(dimension_semantics=("arbitrary",) * 3,
                                             vmem_limit_bytes=VMEM_LIMIT_BYTES),
        name="mla_attention",
    )(q, k, v)


def _ffn_body(x_ref, osb_ref, omla_ref, g_sb_ref, g_mla_ref, w_out_ref, g_ffn_ref, w_up_ref, conv_ref,
              w_down_ref, g_final_ref, out_ref, h_ref, acc_ref, *, tm, ck, tiles_per_seq):
    step = pl.program_id(0)
    halo = BF16_SUBLANES
    d = x_ref.shape[-1]

    @pl.when(step % tiles_per_seq == 0)
    def _():
        h_ref[0:halo, :] = jnp.zeros((halo, d), BF16)

    @pl.when(step % tiles_per_seq != 0)
    def _():
        h_ref[0:halo, :] = h_ref[tm:tm + halo, :]

    o = jnp.concatenate([_rms(osb_ref[...], g_sb_ref[...]), _rms(omla_ref[...], g_mla_ref[...])], axis=-1)
    x2 = x_ref[...] + _dot(o.astype(BF16), w_out_ref[...])
    acc_ref[...] = x2
    h_ref[halo:, :] = _rms(x2, g_ffn_ref[...]).astype(BF16)

    def chunk(c, _):
        u = _dot(h_ref[...], w_up_ref[c])
        cw = conv_ref[c]
        y = u[halo:] * cw[CONV_WIDTH - 1:CONV_WIDTH] + cw[CONV_WIDTH:CONV_WIDTH + 1]
        for back in range(1, CONV_WIDTH):
            tap = cw[CONV_WIDTH - 1 - back:CONV_WIDTH - back]
            y = y + pltpu.roll(u, back, 0)[halo:] * tap
        gate, val = y[:, :ck], y[:, ck:]
        act = gate * (1.0 / (1.0 + jnp.exp(-gate))) * val
        acc_ref[...] += _dot(act.astype(BF16), w_down_ref[c])
        return 0

    lax.fori_loop(0, w_up_ref.shape[0], chunk, 0)
    out_ref[...] = _rms(acc_ref[...], g_final_ref[...])


def _ffn(x2d, osb, omla, g_sb, g_mla, w_out, g_ffn, w_up_c, conv_c, w_down_c, g_final, *, tm, seq):
    t, d = x2d.shape
    n_chunks, _, ck2 = w_up_c.shape
    row = lambda w: pl.BlockSpec((tm, w), lambda i: (i, 0))
    const = lambda a: pl.BlockSpec(a.shape, lambda i: (0,) * a.ndim, pipeline_mode=pl.Buffered(1))
    return pl.pallas_call(
        functools.partial(_ffn_body, tm=tm, ck=ck2 // 2, tiles_per_seq=seq // tm),
        grid=(t // tm,),
        in_specs=[row(d), row(SB_WIDTH), row(MLA_WIDTH), const(g_sb), const(g_mla), const(w_out),
                  const(g_ffn), const(w_up_c), const(conv_c), const(w_down_c), const(g_final)],
        out_specs=row(d),
        out_shape=jax.ShapeDtypeStruct((t, d), F32),
        scratch_shapes=[pltpu.VMEM((tm + BF16_SUBLANES, d), BF16), pltpu.VMEM((tm, d), F32)],
        compiler_params=pltpu.CompilerParams(dimension_semantics=("arbitrary",),
                                             vmem_limit_bytes=VMEM_LIMIT_BYTES),
        name="out_proj_ffn",
    )(x2d, osb, omla, g_sb, g_mla, w_out, g_ffn, w_up_c, conv_c, w_down_c, g_final)


def _tiles(seq):
    return dict(tm_in=256, tq=128, tk=256, tm_ffn=512, ck=256)


def kernel(x, positions, g_mix, w_in, g_cq, w_uq, g_ckv, w_ukv, g_sb_out, g_mla_out, w_out, g_ffn,
           w_up, conv_w, conv_b, w_down, g_final):
    b, s, d = x.shape
    depth = w_in.shape[0]
    t = b * s
    cfg = _tiles(s)
    assert s % cfg["tm_ffn"] == 0 and s % cfg["tk"] == 0 and t % cfg["tm_in"] == 0
    assert D_FF % cfg["ck"] == 0

    inv_freq = 1.0 / (ROPE_BASE ** (jnp.arange(ROPE_HALF, dtype=F32) * (2.0 / MLA_ROPE_DIM)))
    pad_l, pad_r = ROPE_LANE0, LANES - ROPE_LANE0 - MLA_ROPE_DIM
    invf = jnp.pad(jnp.tile(inv_freq, 2), (pad_l, pad_r))[None, :]
    sgn = jnp.pad(jnp.concatenate([-jnp.ones(ROPE_HALF, F32), jnp.ones(ROPE_HALF, F32)]), (pad_l, pad_r))[None, :]
    pos = positions.astype(F32).reshape(t, 1)

    x2d = x.reshape(t, d)
    n_chunks = D_FF // cfg["ck"]
    for l in range(depth):
        n_main = 3 * SB_WIDTH + Q_LORA_RANK + KV_LORA_RANK
        w_in_p = jnp.concatenate(
            [w_in[l][:, :n_main], jnp.pad(w_in[l][:, n_main:], ((0, 0), (pad_l, pad_r)))], axis=1).astype(BF16)
        w_uq_p = jnp.pad(w_uq[l].reshape(Q_LORA_RANK, MLA_HEADS, MLA_QK_DIM),
                         ((0, 0), (0, 0), (0, MLA_HEAD_PAD - MLA_QK_DIM))
                         ).reshape(Q_LORA_RANK, MLA_HEADS * MLA_HEAD_PAD).astype(BF16)
        w_ukv_h = w_ukv[l].reshape(KV_LORA_RANK, MLA_HEADS, MLA_NOPE_DIM + MLA_V_DIM)
        w_uk_p = jnp.pad(w_ukv_h[:, :, :MLA_NOPE_DIM], ((0, 0), (0, 0), (0, MLA_HEAD_PAD - MLA_NOPE_DIM))
                         ).reshape(KV_LORA_RANK, MLA_HEADS * MLA_HEAD_PAD).astype(BF16)
        w_uv = w_ukv_h[:, :, MLA_NOPE_DIM:].reshape(KV_LORA_RANK, MLA_WIDTH).astype(BF16)

        qsb, ksb, vsb, qm, km, vm = _in_proj(
            x2d, pos, g_mix[l][None, :], w_in_p, g_cq[l][None, :], w_uq_p, g_ckv[l][None, :], w_uk_p, w_uv,
            invf, sgn, tm=cfg["tm_in"])

        o_sb = _sb_attention(qsb.reshape(b, s, -1), ksb.reshape(b, s, -1), vsb.reshape(b, s, -1),
                             tq=cfg["tq"], tk=cfg["tk"])
        o_mla = _mla_attention(qm.reshape(b, s, -1), km.reshape(b, s, -1), vm.reshape(b, s, -1),
                               tq=cfg["tq"], tk=cfg["tk"])

        ck = cfg["ck"]
        w_up_c = jnp.concatenate([w_up[l][:, :D_FF].reshape(d, n_chunks, ck),
                                  w_up[l][:, D_FF:].reshape(d, n_chunks, ck)], axis=-1
                                 ).transpose(1, 0, 2).astype(BF16)
        conv_rows = jnp.concatenate([conv_w[l], conv_b[l][None, :]], axis=0)
        conv_c = jnp.concatenate([conv_rows[:, :D_FF].reshape(-1, n_chunks, ck),
                                  conv_rows[:, D_FF:].reshape(-1, n_chunks, ck)], axis=-1).transpose(1, 0, 2)
        conv_c = jnp.pad(conv_c, ((0, 0), (0, 8 - conv_c.shape[1]), (0, 0)))
        w_down_c = w_down[l].reshape(n_chunks, ck, d).astype(BF16)
        last = l == depth - 1
        g_last = g_final[None, :] if last else None
        assert last, "the fused final rmsnorm assumes a single layer"
        x2d = _ffn(x2d, o_sb.reshape(t, -1), o_mla.reshape(t, -1), g_sb_out[l][None, :], g_mla_out[l][None, :],
                   w_out[l].astype(BF16), g_ffn[l][None, :], w_up_c, conv_c, w_down_c, g_last,
                   tm=cfg["tm_ffn"], seq=s)
    return x2d.reshape(b, s, d)
```

```python
import functools

import numpy as np
import jax
import jax.numpy as jnp
from jax import lax
from jax.experimental import pallas as pl
from jax.experimental.pallas import tpu as pltpu

SB_HEADS = 8
SB_HEAD_DIM = 64
MLA_HEADS = 8
MLA_NOPE_DIM = 64
MLA_ROPE_DIM = 32
MLA_V_DIM = 64
Q_LORA_RANK = 384
KV_LORA_RANK = 256
D_FF = 2816
CONV_WIDTH = 3
ROPE_BASE = 10000.0
EPS = 1e-6

SB_WIDTH = SB_HEADS * SB_HEAD_DIM
MLA_WIDTH = MLA_HEADS * MLA_V_DIM
MLA_QK_DIM = MLA_NOPE_DIM + MLA_ROPE_DIM

LANES = 128
BF16_SUBLANES = 16
VMEM_LIMIT_BYTES = 56 * 1024 * 1024

MLA_HEAD_PAD = LANES
ROPE_LANE0 = MLA_NOPE_DIM
ROPE_HALF = MLA_ROPE_DIM // 2

BF16 = jnp.bfloat16
F32 = jnp.float32

LOG2_E = 1.4426950408889634


def _rms(x, g):
    return x * lax.rsqrt(jnp.mean(x * x, axis=-1, keepdims=True) + EPS) * g


def _dot(a, b):
    return jnp.dot(a, b, preferred_element_type=F32)


def _dot_nt(a, b):
    return lax.dot_general(a, b, (((1,), (1,)), ((), ())), preferred_element_type=F32)


def _in_proj_body(x_ref, pos_ref, g_mix_ref, w_in_ref, g_cq_ref, w_uq_ref, g_ckv_ref, w_uk_ref,
                  w_uv_ref, invf_ref, sgn_ref,
                  qsb_ref, ksb_ref, vsb_ref, qm_ref, km_ref, vm_ref):
    h = _rms(x_ref[...], g_mix_ref[...]).astype(BF16)

    c0 = 0
    qsb_ref[...] = (_dot(h, w_in_ref[:, c0:c0 + SB_WIDTH]) * (SB_HEAD_DIM ** -0.5 * LOG2_E)).astype(BF16)
    c0 += SB_WIDTH
    ksb_ref[...] = _dot(h, w_in_ref[:, c0:c0 + SB_WIDTH]).astype(BF16)
    c0 += SB_WIDTH
    vsb_ref[...] = _dot(h, w_in_ref[:, c0:c0 + SB_WIDTH]).astype(BF16)
    c0 += SB_WIDTH
    c_q = _dot(h, w_in_ref[:, c0:c0 + Q_LORA_RANK])
    c0 += Q_LORA_RANK
    c_kv = _dot(h, w_in_ref[:, c0:c0 + KV_LORA_RANK])
    c0 += KV_LORA_RANK
    k_rope = _dot(h, w_in_ref[:, c0:c0 + LANES])

    ang = pos_ref[...] * invf_ref[...]
    cos_t = jnp.cos(ang)
    sin_t = jnp.sin(ang) * sgn_ref[...]
    lane = lax.broadcasted_iota(jnp.int32, (1, LANES), 1)
    first_half = lane < ROPE_LANE0 + ROPE_HALF

    def rope(v):
        partner = jnp.where(first_half, pltpu.roll(v, LANES - ROPE_HALF, 1), pltpu.roll(v, ROPE_HALF, 1))
        return v * cos_t + partner * sin_t

    cqn = _rms(c_q, g_cq_ref[...]).astype(BF16)
    q = _dot(cqn, w_uq_ref[...])
    ckvn = _rms(c_kv, g_ckv_ref[...]).astype(BF16)
    k_nope = _dot(ckvn, w_uk_ref[...])
    vm_ref[...] = _dot(ckvn, w_uv_ref[...]).astype(BF16)
    k_rope = rope(k_rope)
    for hd in range(MLA_HEADS):
        sl = slice(hd * MLA_HEAD_PAD, (hd + 1) * MLA_HEAD_PAD)
        qm_ref[:, sl] = (rope(q[:, sl]) * (MLA_QK_DIM ** -0.5 * LOG2_E)).astype(BF16)
        km_ref[:, sl] = (k_nope[:, sl] + k_rope).astype(BF16)


def _in_proj(x2d, pos, g_mix, w_in_p, g_cq, w_uq_p, g_ckv, w_uk_p, w_uv, invf, sgn, *, tm):
    t, d = x2d.shape
    row = lambda w: pl.BlockSpec((tm, w), lambda i: (i, 0))
    full = lambda a: pl.BlockSpec(a.shape, lambda i: (0, 0))
    outs = [(SB_WIDTH, BF16)] * 3 + [(MLA_HEADS * MLA_HEAD_PAD, BF16)] * 2 + [(MLA_WIDTH, BF16)]
    return pl.pallas_call(
        _in_proj_body,
        grid=(t // tm,),
        in_specs=[row(d), row(1), full(g_mix), full(w_in_p), full(g_cq), full(w_uq_p), full(g_ckv),
                  full(w_uk_p), full(w_uv), full(invf), full(sgn)],
        out_specs=[row(w) for w, _ in outs],
        out_shape=[jax.ShapeDtypeStruct((t, w), dt) for w, dt in outs],
        compiler_params=pltpu.CompilerParams(dimension_semantics=("arbitrary",),
                                             vmem_limit_bytes=VMEM_LIMIT_BYTES),
        name="in_proj",
    )(x2d, pos, g_mix, w_in_p, g_cq, w_uq_p, g_ckv, w_uk_p, w_uv, invf, sgn)


def _sb_body(q_ref, k_ref, v_ref, nsuf_ref, o_ref, tails_ref, acc_ref, *, tq, tk):
    i = pl.program_id(1)
    lane = lax.broadcasted_iota(jnp.int32, (1, LANES), 1)
    lo = lane < SB_HEAD_DIM
    nsuf = nsuf_ref[...]

    row0 = i * tq
    j_diag = row0 // tk
    r_idx = lax.broadcasted_iota(jnp.int32, (tq, tk), 0) + row0
    c_idx = lax.broadcasted_iota(jnp.int32, (tq, tk), 1)

    def tile(j, first):
        k0 = pl.multiple_of(j * tk, tk)
        if first:
            visible = c_idx + k0 < r_idx
        for pair in range(SB_HEADS // 2):
            sl = slice(pair * LANES, (pair + 1) * LANES)
            q = q_ref[0, :, sl]
            kt = k_ref[0, pl.ds(k0, tk), sl]
            vt = v_ref[0, pl.ds(k0, tk), sl]
            zero_q, zero_v = jnp.zeros_like(q), jnp.zeros_like(vt)
            acc = None
            for hd in range(2):
                q_h = jnp.where(lo, q, zero_q) if hd == 0 else jnp.where(lo, zero_q, q)
                v_h = jnp.where(lo, vt, zero_v) if hd == 0 else jnp.where(lo, zero_v, vt)
                z = _dot_nt(q_h, kt)
                keep = jnp.maximum(z, 0.0) + jnp.log2(1.0 + jnp.exp2(-jnp.abs(z)))
                log_beta = z - keep
                if first:
                    keep = jnp.where(visible, keep, 0.0)
                tail = _dot(keep.astype(BF16), nsuf)
                total = jnp.broadcast_to(tail[:, 0:1] - keep[:, 0:1], (tq, LANES))
                if not first:
                    tail = tail + jnp.tile(tails_ref[2 * pair + hd], (1, tk // LANES))
                a = jnp.exp2(log_beta + tail)
                if first:
                    a = jnp.where(visible, a, 0.0)
                    tails_ref[2 * pair + hd] = total
                else:
                    tails_ref[2 * pair + hd] += total
                pv = _dot(a.astype(BF16), v_h)
                acc = pv if acc is None else acc + pv
            if first:
                acc_ref[pair] = acc
            else:
                acc_ref[pair] += acc

    tile(j_diag, True)

    def step(st, _):
        tile(j_diag - 1 - st, False)
        return 0

    lax.fori_loop(0, j_diag, step, 0)
    for pair in range(SB_HEADS // 2):
        o_ref[0, :, pair * LANES:(pair + 1) * LANES] = acc_ref[pair]


def _sb_attention(q, k, v, *, tq, tk):
    b, s, w = q.shape
    idx = np.arange(tk)
    nsuf = jnp.asarray(np.where(idx[:, None] > idx[None, :], -1.0, 0.0), BF16)
    return pl.pallas_call(
        functools.partial(_sb_body, tq=tq, tk=tk),
        grid=(b, s // tq),
        in_specs=[pl.BlockSpec((1, tq, w), lambda bi, i: (bi, i, 0)),
                  pl.BlockSpec((1, s, w), lambda bi, i: (bi, 0, 0)),
                  pl.BlockSpec((1, s, w), lambda bi, i: (bi, 0, 0)),
                  pl.BlockSpec((tk, tk), lambda bi, i: (0, 0))],
        out_specs=pl.BlockSpec((1, tq, w), lambda bi, i: (bi, i, 0)),
        out_shape=jax.ShapeDtypeStruct((b, s, w), F32),
        scratch_shapes=[pltpu.VMEM((SB_HEADS, tq, LANES), F32), pltpu.VMEM((SB_HEADS // 2, tq, LANES), F32)],
        compiler_params=pltpu.CompilerParams(dimension_semantics=("arbitrary",) * 2,
                                             vmem_limit_bytes=VMEM_LIMIT_BYTES),
        name="sb_attention",
    )(q, k, v, nsuf)


def _mla_body(q_ref, k_ref, v_ref, o_ref, m_ref, acc_ref, *, tq, tk):
    i = pl.program_id(1)
    lane = lax.broadcasted_iota(jnp.int32, (1, LANES), 1)
    lo = lane < MLA_V_DIM
    row0 = i * tq
    j_diag = row0 // tk
    r_idx = lax.broadcasted_iota(jnp.int32, (tq, tk), 0) + row0
    c_idx = lax.broadcasted_iota(jnp.int32, (tq, tk), 1)

    def tile(j, first):
        k0 = pl.multiple_of(j * tk, tk)
        if first:
            visible = c_idx + k0 <= r_idx
        for hd in range(MLA_HEADS):
            pair = hd // 2
            q = q_ref[0, :, hd * MLA_HEAD_PAD:(hd + 1) * MLA_HEAD_PAD]
            kt = k_ref[0, pl.ds(k0, tk), hd * MLA_HEAD_PAD:(hd + 1) * MLA_HEAD_PAD]
            vt = v_ref[0, pl.ds(k0, tk), pair * LANES:(pair + 1) * LANES]
            one = jnp.ones_like(vt)
            v_ext = jnp.where(lo, vt, one) if hd % 2 == 0 else jnp.where(lo, one, vt)
            s = _dot_nt(q, kt)
            if first:
                s = jnp.where(visible, s, -jnp.inf)
                m_new = jnp.broadcast_to(jnp.max(s, axis=-1, keepdims=True), (tq, LANES))
            else:
                m_old = m_ref[hd]
                m_new = jnp.maximum(m_old, jnp.max(s, axis=-1, keepdims=True))
            p = jnp.exp2(s - jnp.tile(m_new, (1, tk // LANES)))
            pv = _dot(p.astype(BF16), v_ext)
            if first:
                acc_ref[hd] = pv
            else:
                acc_ref[hd] = jnp.exp2(m_old - m_new) * acc_ref[hd] + pv
            m_ref[hd] = m_new

    tile(j_diag, True)

    def step(st, _):
        tile(st, False)
        return 0

    lax.fori_loop(0, j_diag, step, 0)
    for pair in range(MLA_HEADS // 2):
        a0, a1 = acc_ref[2 * pair], acc_ref[2 * pair + 1]
        o_ref[0, :, pair * LANES:(pair + 1) * LANES] = jnp.where(
            lo, a0 / pltpu.roll(a0, MLA_V_DIM, 1), a1 / pltpu.roll(a1, MLA_V_DIM, 1))


def _mla_attention(q, k, v, *, tq, tk):
    b, s, wq = q.shape
    w = v.shape[-1]
    return pl.pallas_call(
        functools.partial(_mla_body, tq=tq, tk=tk),
        grid=(b, s // tq),
        in_specs=[pl.BlockSpec((1, tq, wq), lambda bi, i: (bi, i, 0)),
                  pl.BlockSpec((1, s, wq), lambda bi, i: (bi, 0, 0)),
                  pl.BlockSpec((1, s, w), lambda bi, i: (bi, 0, 0))],
        out_specs=pl.BlockSpec((1, tq, w), lambda bi, i: (bi, i, 0)),
        out_shape=jax.ShapeDtypeStruct((b, s, w), F32),
        scratch_shapes=[pltpu.VMEM((MLA_HEADS, tq, LANES), F32), pltpu.VMEM((MLA_HEADS, tq, LANES), F32)],
        compiler_params=pltpu.CompilerParams(dimension_semantics=("arbitrary",) * 2,
                                             vmem_limit_bytes=VMEM_LIMIT_BYTES),
        name="mla_attention",
    )(q, k, v)


def _ffn_body(x_ref, osb_ref, omla_ref, g_sb_ref, g_mla_ref, w_out_ref, g_ffn_ref, w_up_ref, conv_ref,
              w_down_ref, g_final_ref, out_ref, h_ref, acc_ref, *, tm, ck, tiles_per_seq):
    step = pl.program_id(0)
    halo = BF16_SUBLANES
    d = x_ref.shape[-1]

    @pl.when(step % tiles_per_seq == 0)
    def _():
        h_ref[0:halo, :] = jnp.zeros((halo, d), BF16)

    @pl.when(step % tiles_per_seq != 0)
    def _():
        h_ref[0:halo, :] = h_ref[tm:tm + halo, :]

    o = jnp.concatenate([_rms(osb_ref[...], g_sb_ref[...]), _rms(omla_ref[...], g_mla_ref[...])], axis=-1)
    x2 = x_ref[...] + _dot(o.astype(BF16), w_out_ref[...])
    acc_ref[...] = x2
    h_ref[halo:, :] = _rms(x2, g_ffn_ref[...]).astype(BF16)

    def chunk(c, _):
        u = _dot(h_ref[...], w_up_ref[c])
        cw = conv_ref[c]
        y = u[halo:] * cw[CONV_WIDTH - 1:CONV_WIDTH] + cw[CONV_WIDTH:CONV_WIDTH + 1]
        for back in range(1, CONV_WIDTH):
            tap = cw[CONV_WIDTH - 1 - back:CONV_WIDTH - back]
            y = y + pltpu.roll(u, back, 0)[halo:] * tap
        gate, val = y[:, :ck], y[:, ck:]
        act = gate * (1.0 / (1.0 + jnp.exp(-gate))) * val
        acc_ref[...] += _dot(act.astype(BF16), w_down_ref[c])
        return 0

    lax.fori_loop(0, w_up_ref.shape[0], chunk, 0)
    out_ref[...] = _rms(acc_ref[...], g_final_ref[...])


def _ffn(x2d, osb, omla, g_sb, g_mla, w_out, g_ffn, w_up_c, conv_c, w_down_c, g_final, *, tm, seq):
    t, d = x2d.shape
    n_chunks, _, ck2 = w_up_c.shape
    row = lambda w: pl.BlockSpec((tm, w), lambda i: (i, 0))
    const = lambda a: pl.BlockSpec(a.shape, lambda i: (0,) * a.ndim, pipeline_mode=pl.Buffered(1))
    return pl.pallas_call(
        functools.partial(_ffn_body, tm=tm, ck=ck2 // 2, tiles_per_seq=seq // tm),
        grid=(t // tm,),
        in_specs=[row(d), row(SB_WIDTH), row(MLA_WIDTH), const(g_sb), const(g_mla), const(w_out),
                  const(g_ffn), const(w_up_c), const(conv_c), const(w_down_c), const(g_final)],
        out_specs=row(d),
        out_shape=jax.ShapeDtypeStruct((t, d), F32),
        scratch_shapes=[pltpu.VMEM((tm + BF16_SUBLANES, d), BF16), pltpu.VMEM((tm, d), F32)],
        compiler_params=pltpu.CompilerParams(dimension_semantics=("arbitrary",),
                                             vmem_limit_bytes=VMEM_LIMIT_BYTES),
        name="out_proj_ffn",
    )(x2d, osb, omla, g_sb, g_mla, w_out, g_ffn, w_up_c, conv_c, w_down_c, g_final)


def _tiles(seq):
    return dict(tm_in=256, tq=256, tk=256, tm_ffn=512, ck=256)


def kernel(x, positions, g_mix, w_in, g_cq, w_uq, g_ckv, w_ukv, g_sb_out, g_mla_out, w_out, g_ffn,
           w_up, conv_w, conv_b, w_down, g_final):
    b, s, d = x.shape
    depth = w_in.shape[0]
    t = b * s
    cfg = _tiles(s)
    assert s % cfg["tm_ffn"] == 0 and s % cfg["tk"] == 0 and t % cfg["tm_in"] == 0
    assert D_FF % cfg["ck"] == 0

    inv_freq = 1.0 / (ROPE_BASE ** (jnp.arange(ROPE_HALF, dtype=F32) * (2.0 / MLA_ROPE_DIM)))
    pad_l, pad_r = ROPE_LANE0, LANES - ROPE_LANE0 - MLA_ROPE_DIM
    invf = jnp.pad(jnp.tile(inv_freq, 2), (pad_l, pad_r))[None, :]
    sgn = jnp.pad(jnp.concatenate([-jnp.ones(ROPE_HALF, F32), jnp.ones(ROPE_HALF, F32)]), (pad_l, pad_r))[None, :]
    pos = positions.astype(F32).reshape(t, 1)

    x2d = x.reshape(t, d)
    n_chunks = D_FF // cfg["ck"]
    for l in range(depth):
        n_main = 3 * SB_WIDTH + Q_LORA_RANK + KV_LORA_RANK
        w_in_p = jnp.concatenate(
            [w_in[l][:, :n_main], jnp.pad(w_in[l][:, n_main:], ((0, 0), (pad_l, pad_r)))], axis=1).astype(BF16)
        w_uq_p = jnp.pad(w_uq[l].reshape(Q_LORA_RANK, MLA_HEADS, MLA_QK_DIM),
                         ((0, 0), (0, 0), (0, MLA_HEAD_PAD - MLA_QK_DIM))
                         ).reshape(Q_LORA_RANK, MLA_HEADS * MLA_HEAD_PAD).astype(BF16)
        w_ukv_h = w_ukv[l].reshape(KV_LORA_RANK, MLA_HEADS, MLA_NOPE_DIM + MLA_V_DIM)
        w_uk_p = jnp.pad(w_ukv_h[:, :, :MLA_NOPE_DIM], ((0, 0), (0, 0), (0, MLA_HEAD_PAD - MLA_NOPE_DIM))
                         ).reshape(KV_LORA_RANK, MLA_HEADS * MLA_HEAD_PAD).astype(BF16)
        w_uv = w_ukv_h[:, :, MLA_NOPE_DIM:].reshape(KV_LORA_RANK, MLA_WIDTH).astype(BF16)

        qsb, ksb, vsb, qm, km, vm = _in_proj(
            x2d, pos, g_mix[l][None, :], w_in_p, g_cq[l][None, :], w_uq_p, g_ckv[l][None, :], w_uk_p, w_uv,
            invf, sgn, tm=cfg["tm_in"])

        o_sb = _sb_attention(qsb.reshape(b, s, -1), ksb.reshape(b, s, -1), vsb.reshape(b, s, -1),
                             tq=cfg["tq"], tk=cfg["tk"])
        o_mla = _mla_attention(qm.reshape(b, s, -1), km.reshape(b, s, -1), vm.reshape(b, s, -1),
                               tq=cfg["tq"], tk=cfg["tk"])

        ck = cfg["ck"]
        w_up_c = jnp.concatenate([w_up[l][:, :D_FF].reshape(d, n_chunks, ck),
                                  w_up[l][:, D_FF:].reshape(d, n_chunks, ck)], axis=-1
                                 ).transpose(1, 0, 2).astype(BF16)
        conv_rows = jnp.concatenate([conv_w[l], conv_b[l][None, :]], axis=0)
        conv_c = jnp.concatenate([conv_rows[:, :D_FF].reshape(-1, n_chunks, ck),
                                  conv_rows[:, D_FF:].reshape(-1, n_chunks, ck)], axis=-1).transpose(1, 0, 2)
        conv_c = jnp.pad(conv_c, ((0, 0), (0, 8 - conv_c.shape[1]), (0, 0)))
        w_down_c = w_down[l].reshape(n_chunks, ck, d).astype(BF16)
        last = l == depth - 1
        g_last = g_final[None, :] if last else None
        assert last, "the fused final rmsnorm assumes a single layer"
        x2d = _ffn(x2d, o_sb.reshape(t, -1), o_mla.reshape(t, -1), g_sb_out[l][None, :], g_mla_out[l][None, :],
                   w_out[l].astype(BF16), g_ffn[l][None, :], w_up_c, conv_c, w_down_c, g_last,
                   tm=cfg["tm_ffn"], seq=s)
    return x2d.reshape(b, s, d)
```

```python
import functools

import numpy as np
import jax
import jax.numpy as jnp
from jax import lax
from jax.experimental import pallas as pl
from jax.experimental.pallas import tpu as pltpu

SB_HEADS = 8
SB_HEAD_DIM = 64
MLA_HEADS = 8
MLA_NOPE_DIM = 64
MLA_ROPE_DIM = 32
MLA_V_DIM = 64
Q_LORA_RANK = 384
KV_LORA_RANK = 256
D_FF = 2816
CONV_WIDTH = 3
ROPE_BASE = 10000.0
EPS = 1e-6

SB_WIDTH = SB_HEADS * SB_HEAD_DIM
MLA_WIDTH = MLA_HEADS * MLA_V_DIM
MLA_QK_DIM = MLA_NOPE_DIM + MLA_ROPE_DIM

LANES = 128
BF16_SUBLANES = 16
VMEM_LIMIT_BYTES = 56 * 1024 * 1024

MLA_HEAD_PAD = LANES
ROPE_LANE0 = MLA_NOPE_DIM
ROPE_HALF = MLA_ROPE_DIM // 2

BF16 = jnp.bfloat16
F32 = jnp.float32

LOG2_E = 1.4426950408889634
SB_DEAD_LOG2 = -152.0


def _rms(x, g):
    return x * lax.rsqrt(jnp.mean(x * x, axis=-1, keepdims=True) + EPS) * g


def _dot(a, b):
    return jnp.dot(a, b, preferred_element_type=F32)


def _dot_nt(a, b):
    return lax.dot_general(a, b, (((1,), (1,)), ((), ())), preferred_element_type=F32)


def _in_proj_body(x_ref, pos_ref, g_mix_ref, w_in_ref, g_cq_ref, w_uq_ref, g_ckv_ref, w_uk_ref,
                  w_uv_ref, invf_ref, sgn_ref,
                  qsb_ref, ksb_ref, vsb_ref, qm_ref, km_ref, vm_ref):
    h = _rms(x_ref[...], g_mix_ref[...]).astype(BF16)

    c0 = 0
    qsb_ref[...] = (_dot(h, w_in_ref[:, c0:c0 + SB_WIDTH]) * (SB_HEAD_DIM ** -0.5 * LOG2_E)).astype(BF16)
    c0 += SB_WIDTH
    ksb_ref[...] = _dot(h, w_in_ref[:, c0:c0 + SB_WIDTH]).astype(BF16)
    c0 += SB_WIDTH
    vsb_ref[...] = _dot(h, w_in_ref[:, c0:c0 + SB_WIDTH]).astype(BF16)
    c0 += SB_WIDTH
    c_q = _dot(h, w_in_ref[:, c0:c0 + Q_LORA_RANK])
    c0 += Q_LORA_RANK
    c_kv = _dot(h, w_in_ref[:, c0:c0 + KV_LORA_RANK])
    c0 += KV_LORA_RANK
    k_rope = _dot(h, w_in_ref[:, c0:c0 + LANES])

    ang = pos_ref[...] * invf_ref[...]
    cos_t = jnp.cos(ang)
    sin_t = jnp.sin(ang) * sgn_ref[...]
    lane = lax.broadcasted_iota(jnp.int32, (1, LANES), 1)
    first_half = lane < ROPE_LANE0 + ROPE_HALF

    def rope(v):
        partner = jnp.where(first_half, pltpu.roll(v, LANES - ROPE_HALF, 1), pltpu.roll(v, ROPE_HALF, 1))
        return v * cos_t + partner * sin_t

    cqn = _rms(c_q, g_cq_ref[...]).astype(BF16)
    q = _dot(cqn, w_uq_ref[...])
    ckvn = _rms(c_kv, g_ckv_ref[...]).astype(BF16)
    k_nope = _dot(ckvn, w_uk_ref[...])
    vm_ref[...] = _dot(ckvn, w_uv_ref[...]).astype(BF16)
    k_rope = rope(k_rope)
    for hd in range(MLA_HEADS):
        sl = slice(hd * MLA_HEAD_PAD, (hd + 1) * MLA_HEAD_PAD)
        qm_ref[:, sl] = (rope(q[:, sl]) * (MLA_QK_DIM ** -0.5 * LOG2_E)).astype(BF16)
        km_ref[:, sl] = (k_nope[:, sl] + k_rope).astype(BF16)


def _in_proj(x2d, pos, g_mix, w_in_p, g_cq, w_uq_p, g_ckv, w_uk_p, w_uv, invf, sgn, *, tm):
    t, d = x2d.shape
    row = lambda w: pl.BlockSpec((tm, w), lambda i: (i, 0))
    full = lambda a: pl.BlockSpec(a.shape, lambda i: (0, 0))
    outs = [(SB_WIDTH, BF16)] * 3 + [(MLA_HEADS * MLA_HEAD_PAD, BF16)] * 2 + [(MLA_WIDTH, BF16)]
    return pl.pallas_call(
        _in_proj_body,
        grid=(t // tm,),
        in_specs=[row(d), row(1), full(g_mix), full(w_in_p), full(g_cq), full(w_uq_p), full(g_ckv),
                  full(w_uk_p), full(w_uv), full(invf), full(sgn)],
        out_specs=[row(w) for w, _ in outs],
        out_shape=[jax.ShapeDtypeStruct((t, w), dt) for w, dt in outs],
        compiler_params=pltpu.CompilerParams(dimension_semantics=("arbitrary",),
                                             vmem_limit_bytes=VMEM_LIMIT_BYTES),
        name="in_proj",
    )(x2d, pos, g_mix, w_in_p, g_cq, w_uq_p, g_ckv, w_uk_p, w_uv, invf, sgn)


def _sb_body(q_ref, k_ref, v_ref, nsuf_ref, o_ref, qh_ref, tails_ref, acc_ref, *, tq):
    i = pl.program_id(1)
    n_pairs = SB_HEADS // 2
    lane = lax.broadcasted_iota(jnp.int32, (1, LANES), 1)
    lo = lane < SB_HEAD_DIM
    nsuf = nsuf_ref[...]
    visible = (lax.broadcasted_iota(jnp.int32, (tq, tq), 1) < lax.broadcasted_iota(jnp.int32, (tq, tq), 0))

    for pair in range(n_pairs):
        q = q_ref[0, :, pair * LANES:(pair + 1) * LANES]
        zero_q = jnp.zeros_like(q)
        qh_ref[2 * pair] = jnp.where(lo, q, zero_q)
        qh_ref[2 * pair + 1] = jnp.where(lo, zero_q, q)

    def load_kv(j):
        k0 = pl.multiple_of(j * tq, tq)
        return [(k_ref[0, pl.ds(k0, tq), pair * LANES:(pair + 1) * LANES],
                 v_ref[0, pl.ds(k0, tq), pair * LANES:(pair + 1) * LANES]) for pair in range(n_pairs)]

    def tile(kv, state):
        diag = state is None
        tails, accs = ([None] * SB_HEADS, [None] * n_pairs) if diag else state
        new_tails, new_accs = [], []
        for pair in range(n_pairs):
            kt, vt = kv[pair]
            zero_v = jnp.zeros_like(vt)
            acc = accs[pair]
            for hd in range(2):
                h = 2 * pair + hd
                v_h = jnp.where(lo, vt, zero_v) if hd == 0 else jnp.where(lo, zero_v, vt)
                z = _dot_nt(qh_ref[h], kt)
                keep = jnp.maximum(z, 0.0) + jnp.log2(1.0 + jnp.exp2(-jnp.abs(z)))
                log_beta = z - keep
                if diag:
                    keep = jnp.where(visible, keep, 0.0)
                tail = _dot(keep.astype(BF16), nsuf)
                total = jnp.broadcast_to(tail[:, 0:1] - keep[:, 0:1], (tq, LANES))
                if not diag:
                    tail = tail + jnp.tile(tails[h], (1, tq // LANES))
                a = jnp.exp2(log_beta + tail)
                if diag:
                    a = jnp.where(visible, a, 0.0)
                new_tails.append(total if diag else tails[h] + total)
                pv = _dot(a.astype(BF16), v_h)
                acc = pv if acc is None else acc + pv
            new_accs.append(acc)
        return new_tails, new_accs

    def save(state):
        tails, accs = state
        for h in range(SB_HEADS):
            tails_ref[h] = tails[h]
        for pair in range(n_pairs):
            acc_ref[pair] = accs[pair]

    @pl.when(i == 0)
    def _():
        save(tile(load_kv(0), None))

    @pl.when(i > 0)
    def _():
        kv_diag, kv_prev = load_kv(i), load_kv(i - 1)
        save(tile(kv_prev, tile(kv_diag, None)))

    def alive():
        return jnp.max(tails_ref[...]) > SB_DEAD_LOG2

    def sweep(carry):
        j, _ = carry
        kv = load_kv(j)
        state = ([tails_ref[h] for h in range(SB_HEADS)], [acc_ref[pair] for pair in range(n_pairs)])
        save(tile(kv, state))
        return j - 1, alive()

    @pl.when(i > 1)
    def _():
        lax.while_loop(lambda c: (c[0] >= 0) & c[1], sweep, (i - 2, alive()))

    for pair in range(n_pairs):
        o_ref[0, :, pair * LANES:(pair + 1) * LANES] = acc_ref[pair]


def _sb_attention(q, k, v, *, tq):
    b, s, w = q.shape
    idx = np.arange(tq)
    nsuf = jnp.asarray(np.where(idx[:, None] > idx[None, :], -1.0, 0.0), BF16)
    return pl.pallas_call(
        functools.partial(_sb_body, tq=tq),
        grid=(b, s // tq),
        in_specs=[pl.BlockSpec((1, tq, w), lambda bi, i: (bi, i, 0)),
                  pl.BlockSpec((1, s, w), lambda bi, i: (bi, 0, 0)),
                  pl.BlockSpec((1, s, w), lambda bi, i: (bi, 0, 0)),
                  pl.BlockSpec((tq, tq), lambda bi, i: (0, 0))],
        out_specs=pl.BlockSpec((1, tq, w), lambda bi, i: (bi, i, 0)),
        out_shape=jax.ShapeDtypeStruct((b, s, w), F32),
        scratch_shapes=[pltpu.VMEM((SB_HEADS, tq, LANES), BF16), pltpu.VMEM((SB_HEADS, tq, LANES), F32),
                        pltpu.VMEM((SB_HEADS // 2, tq, LANES), F32)],
        compiler_params=pltpu.CompilerParams(dimension_semantics=("arbitrary",) * 2,
                                             vmem_limit_bytes=VMEM_LIMIT_BYTES),
        name="sb_attention",
    )(q, k, v, nsuf)


def _mla_body(q_ref, k_ref, v_ref, o_ref, *, tq, n_q):
    i = pl.program_id(1)
    lane = lax.broadcasted_iota(jnp.int32, (1, LANES), 1)
    lo = lane < MLA_V_DIM
    visible = (lax.broadcasted_iota(jnp.int32, (tq, tq), 1) <= lax.broadcasted_iota(jnp.int32, (tq, tq), 0))

    def run(n_tiles):
        m = [None] * MLA_HEADS
        acc = [None] * MLA_HEADS
        for j in [n_tiles - 1] + list(range(n_tiles - 1)):
            rows = slice(j * tq, (j + 1) * tq)
            for hd in range(MLA_HEADS):
                pair = hd // 2
                q = q_ref[0, :, hd * MLA_HEAD_PAD:(hd + 1) * MLA_HEAD_PAD]
                kt = k_ref[0, rows, hd * MLA_HEAD_PAD:(hd + 1) * MLA_HEAD_PAD]
                vt = v_ref[0, rows, pair * LANES:(pair + 1) * LANES]
                one = jnp.ones_like(vt)
                v_ext = jnp.where(lo, vt, one) if hd % 2 == 0 else jnp.where(lo, one, vt)
                s = _dot_nt(q, kt)
                if m[hd] is None:
                    s = jnp.where(visible, s, -jnp.inf)
                    m_new = jnp.broadcast_to(jnp.max(s, axis=-1, keepdims=True), (tq, LANES))
                else:
                    m_new = jnp.maximum(m[hd], jnp.max(s, axis=-1, keepdims=True))
                p = jnp.exp2(s - jnp.tile(m_new, (1, tq // LANES)))
                pv = _dot(p.astype(BF16), v_ext)
                acc[hd] = pv if m[hd] is None else jnp.exp2(m[hd] - m_new) * acc[hd] + pv
                m[hd] = m_new
        for pair in range(MLA_HEADS // 2):
            a0, a1 = acc[2 * pair], acc[2 * pair + 1]
            o_ref[0, :, pair * LANES:(pair + 1) * LANES] = jnp.where(
                lo, a0 / pltpu.roll(a0, MLA_V_DIM, 1), a1 / pltpu.roll(a1, MLA_V_DIM, 1))

    for c in range(n_q):
        pl.when(i == c)(functools.partial(run, c + 1))


def _mla_attention(q, k, v, *, tq):
    b, s, wq = q.shape
    w = v.shape[-1]
    return pl.pallas_call(
        functools.partial(_mla_body, tq=tq, n_q=s // tq),
        grid=(b, s // tq),
        in_specs=[pl.BlockSpec((1, tq, wq), lambda bi, i: (bi, i, 0)),
                  pl.BlockSpec((1, s, wq), lambda bi, i: (bi, 0, 0)),
                  pl.BlockSpec((1, s, w), lambda bi, i: (bi, 0, 0))],
        out_specs=pl.BlockSpec((1, tq, w), lambda bi, i: (bi, i, 0)),
        out_shape=jax.ShapeDtypeStruct((b, s, w), F32),
        compiler_params=pltpu.CompilerParams(dimension_semantics=("arbitrary",) * 2,
                                             vmem_limit_bytes=VMEM_LIMIT_BYTES),
        name="mla_attention",
    )(q, k, v)


def _ffn_body(x_ref, osb_ref, omla_ref, g_sb_ref, g_mla_ref, w_out_ref, g_ffn_ref, w_up_ref, conv_ref,
              w_down_ref, g_final_ref, out_ref, h_ref, acc_ref, u_ref, *, tm, ck, tiles_per_seq):
    step = pl.program_id(0)
    halo = BF16_SUBLANES
    d = x_ref.shape[-1]

    @pl.when(step % tiles_per_seq == 0)
    def _():
        h_ref[0:halo, :] = jnp.zeros((halo, d), BF16)

    @pl.when(step % tiles_per_seq != 0)
    def _():
        h_ref[0:halo, :] = h_ref[tm:tm + halo, :]

    o = jnp.concatenate([_rms(osb_ref[...], g_sb_ref[...]), _rms(omla_ref[...], g_mla_ref[...])], axis=-1)
    x2 = x_ref[...] + _dot(o.astype(BF16), w_out_ref[...])
    acc_ref[...] = x2
    h_ref[halo:, :] = _rms(x2, g_ffn_ref[...]).astype(BF16)

    def up(c, slot):
        u_ref[slot] = _dot(h_ref[...], w_up_ref[c])

    def down(c, slot):
        u = u_ref[slot]
        cw = conv_ref[c]
        y = u[halo:] * cw[CONV_WIDTH - 1:CONV_WIDTH] + cw[CONV_WIDTH:CONV_WIDTH + 1]
        for back in range(1, CONV_WIDTH):
            tap = cw[CONV_WIDTH - 1 - back:CONV_WIDTH - back]
            y = y + pltpu.roll(u, back, 0)[halo:] * tap
        gate, val = y[:, :ck], y[:, ck:]
        act = gate * (1.0 / (1.0 + jnp.exp(-gate))) * val
        acc_ref[...] += _dot(act.astype(BF16), w_down_ref[c])

    n_chunks = w_up_ref.shape[0]
    up(0, 0)

    def two_chunks(p, _):
        up(2 * p + 1, 1)
        down(2 * p, 0)
        up(2 * p + 2, 0)
        down(2 * p + 1, 1)
        return 0

    lax.fori_loop(0, (n_chunks - 1) // 2, two_chunks, 0)
    down(n_chunks - 1, 0)
    out_ref[...] = _rms(acc_ref[...], g_final_ref[...])


def _ffn(x2d, osb, omla, g_sb, g_mla, w_out, g_ffn, w_up_c, conv_c, w_down_c, g_final, *, tm, seq):
    t, d = x2d.shape
    n_chunks, _, ck2 = w_up_c.shape
    row = lambda w: pl.BlockSpec((tm, w), lambda i: (i, 0))
    const = lambda a: pl.BlockSpec(a.shape, lambda i: (0,) * a.ndim, pipeline_mode=pl.Buffered(1))
    return pl.pallas_call(
        functools.partial(_ffn_body, tm=tm, ck=ck2 // 2, tiles_per_seq=seq // tm),
        grid=(t // tm,),
        in_specs=[row(d), row(SB_WIDTH), row(MLA_WIDTH), const(g_sb), const(g_mla), const(w_out),
                  const(g_ffn), const(w_up_c), const(conv_c), const(w_down_c), const(g_final)],
        out_specs=row(d),
        out_shape=jax.ShapeDtypeStruct((t, d), F32),
        scratch_shapes=[pltpu.VMEM((tm + BF16_SUBLANES, d), BF16), pltpu.VMEM((tm, d), F32),
                        pltpu.VMEM((2, tm + BF16_SUBLANES, ck2), F32)],
        compiler_params=pltpu.CompilerParams(dimension_semantics=("arbitrary",),
                                             vmem_limit_bytes=VMEM_LIMIT_BYTES),
        name="out_proj_ffn",
    )(x2d, osb, omla, g_sb, g_mla, w_out, g_ffn, w_up_c, conv_c, w_down_c, g_final)


def _tiles(seq):
    return dict(tm_in=256, tq=256, tm_ffn=512, ck=256)


def kernel(x, positions, g_mix, w_in, g_cq, w_uq, g_ckv, w_ukv, g_sb_out, g_mla_out, w_out, g_ffn,
           w_up, conv_w, conv_b, w_down, g_final):
    b, s, d = x.shape
    depth = w_in.shape[0]
    t = b * s
    cfg = _tiles(s)
    assert s % cfg["tm_ffn"] == 0 and s % cfg["tq"] == 0 and t % cfg["tm_in"] == 0
    assert D_FF % cfg["ck"] == 0 and (D_FF // cfg["ck"]) % 2 == 1

    inv_freq = 1.0 / (ROPE_BASE ** (jnp.arange(ROPE_HALF, dtype=F32) * (2.0 / MLA_ROPE_DIM)))
    pad_l, pad_r = ROPE_LANE0, LANES - ROPE_LANE0 - MLA_ROPE_DIM
    invf = jnp.pad(jnp.tile(inv_freq, 2), (pad_l, pad_r))[None, :]
    sgn = jnp.pad(jnp.concatenate([-jnp.ones(ROPE_HALF, F32), jnp.ones(ROPE_HALF, F32)]), (pad_l, pad_r))[None, :]
    pos = positions.astype(F32).reshape(t, 1)

    x2d = x.reshape(t, d)
    n_chunks = D_FF // cfg["ck"]
    for l in range(depth):
        n_main = 3 * SB_WIDTH + Q_LORA_RANK + KV_LORA_RANK
        w_in_p = jnp.concatenate(
            [w_in[l][:, :n_main], jnp.pad(w_in[l][:, n_main:], ((0, 0), (pad_l, pad_r)))], axis=1).astype(BF16)
        w_uq_p = jnp.pad(w_uq[l].reshape(Q_LORA_RANK, MLA_HEADS, MLA_QK_DIM),
                         ((0, 0), (0, 0), (0, MLA_HEAD_PAD - MLA_QK_DIM))
                         ).reshape(Q_LORA_RANK, MLA_HEADS * MLA_HEAD_PAD).astype(BF16)
        w_ukv_h = w_ukv[l].reshape(KV_LORA_RANK, MLA_HEADS, MLA_NOPE_DIM + MLA_V_DIM)
        w_uk_p = jnp.pad(w_ukv_h[:, :, :MLA_NOPE_DIM], ((0, 0), (0, 0), (0, MLA_HEAD_PAD - MLA_NOPE_DIM))
                         ).reshape(KV_LORA_RANK, MLA_HEADS * MLA_HEAD_PAD).astype(BF16)
        w_uv = w_ukv_h[:, :, MLA_NOPE_DIM:].reshape(KV_LORA_RANK, MLA_WIDTH).astype(BF16)

        qsb, ksb, vsb, qm, km, vm = _in_proj(
            x2d, pos, g_mix[l][None, :], w_in_p, g_cq[l][None, :], w_uq_p, g_ckv[l][None, :], w_uk_p, w_uv,
            invf, sgn, tm=cfg["tm_in"])

        o_sb = _sb_attention(qsb.reshape(b, s, -1), ksb.reshape(b, s, -1), vsb.reshape(b, s, -1), tq=cfg["tq"])
        o_mla = _mla_attention(qm.reshape(b, s, -1), km.reshape(b, s, -1), vm.reshape(b, s, -1), tq=cfg["tq"])

        ck = cfg["ck"]
        w_up_c = jnp.concatenate([w_up[l][:, :D_FF].reshape(d, n_chunks, ck),
                                  w_up[l][:, D_FF:].reshape(d, n_chunks, ck)], axis=-1
                                 ).transpose(1, 0, 2).astype(BF16)
        conv_rows = jnp.concatenate([conv_w[l], conv_b[l][None, :]], axis=0)
        conv_c = jnp.concatenate([conv_rows[:, :D_FF].reshape(-1, n_chunks, ck),
                                  conv_rows[:, D_FF:].reshape(-1, n_chunks, ck)], axis=-1).transpose(1, 0, 2)
        conv_c = jnp.pad(conv_c, ((0, 0), (0, 8 - conv_c.shape[1]), (0, 0)))
        w_down_c = w_down[l].reshape(n_chunks, ck, d).astype(BF16)
        last = l == depth - 1
        g_last = g_final[None, :] if last else None
        assert last, "the fused final rmsnorm assumes a single layer"
        x2d = _ffn(x2d, o_sb.reshape(t, -1), o_mla.reshape(t, -1), g_sb_out[l][None, :], g_mla_out[l][None, :],
                   w_out[l].astype(BF16), g_ffn[l][None, :], w_up_c, conv_c, w_down_c, g_last,
                   tm=cfg["tm_ffn"], seq=s)
    return x2d.reshape(b, s, d)
```

```python
import functools

import numpy as np
import jax
import jax.numpy as jnp
from jax import lax
from jax.experimental import pallas as pl
from jax.experimental.pallas import tpu as pltpu

SB_HEADS = 8
SB_HEAD_DIM = 64
MLA_HEADS = 8
MLA_NOPE_DIM = 64
MLA_ROPE_DIM = 32
MLA_V_DIM = 64
Q_LORA_RANK = 384
KV_LORA_RANK = 256
D_FF = 2816
CONV_WIDTH = 3
ROPE_BASE = 10000.0
EPS = 1e-6

SB_WIDTH = SB_HEADS * SB_HEAD_DIM
MLA_WIDTH = MLA_HEADS * MLA_V_DIM
MLA_QK_DIM = MLA_NOPE_DIM + MLA_ROPE_DIM

LANES = 128
BF16_SUBLANES = 16
VMEM_LIMIT_BYTES = 56 * 1024 * 1024

MLA_HEAD_PAD = LANES
ROPE_LANE0 = MLA_NOPE_DIM
ROPE_HALF = MLA_ROPE_DIM // 2

BF16 = jnp.bfloat16
F32 = jnp.float32

LOG2_E = 1.4426950408889634
SB_DEAD_LOG2 = -152.0


def _rms(x, g):
    return x * lax.rsqrt(jnp.mean(x * x, axis=-1, keepdims=True) + EPS) * g


def _dot(a, b):
    return jnp.dot(a, b, preferred_element_type=F32)


def _dot_nt(a, b):
    return lax.dot_general(a, b, (((1,), (1,)), ((), ())), preferred_element_type=F32)


def _emit_skewed(items, stages, skew=1):
    for step in range(len(items) + (len(stages) - 1) * skew):
        for k, stage in enumerate(stages):
            n = step - k * skew
            if 0 <= n < len(items):
                stage(items[n])


def _in_proj_body(x_ref, pos_ref, g_mix_ref, w_in_ref, g_cq_ref, w_uq_ref, g_ckv_ref, w_uk_ref,
                  w_uv_ref, invf_ref, sgn_ref,
                  qsb_ref, ksb_ref, vsb_ref, qm_ref, km_ref, vm_ref):
    h = _rms(x_ref[...], g_mix_ref[...]).astype(BF16)

    c0 = 3 * SB_WIDTH
    c_q = _dot(h, w_in_ref[:, c0:c0 + Q_LORA_RANK])
    c0 += Q_LORA_RANK
    c_kv = _dot(h, w_in_ref[:, c0:c0 + KV_LORA_RANK])
    c0 += KV_LORA_RANK
    k_rope = _dot(h, w_in_ref[:, c0:c0 + LANES])
    qsb_ref[...] = (_dot(h, w_in_ref[:, 0:SB_WIDTH]) * (SB_HEAD_DIM ** -0.5 * LOG2_E)).astype(BF16)
    ksb_ref[...] = _dot(h, w_in_ref[:, SB_WIDTH:2 * SB_WIDTH]).astype(BF16)
    vsb_ref[...] = _dot(h, w_in_ref[:, 2 * SB_WIDTH:3 * SB_WIDTH]).astype(BF16)

    ang = pos_ref[...] * invf_ref[...]
    cos_t = jnp.cos(ang)
    sin_t = jnp.sin(ang) * sgn_ref[...]
    lane = lax.broadcasted_iota(jnp.int32, (1, LANES), 1)
    first_half = lane < ROPE_LANE0 + ROPE_HALF

    def rope(v):
        partner = jnp.where(first_half, pltpu.roll(v, LANES - ROPE_HALF, 1), pltpu.roll(v, ROPE_HALF, 1))
        return v * cos_t + partner * sin_t

    cqn = _rms(c_q, g_cq_ref[...]).astype(BF16)
    q = _dot(cqn, w_uq_ref[...])
    ckvn = _rms(c_kv, g_ckv_ref[...]).astype(BF16)
    k_nope = _dot(ckvn, w_uk_ref[...])
    vm_ref[...] = _dot(ckvn, w_uv_ref[...]).astype(BF16)
    k_rope = rope(k_rope)
    for hd in range(MLA_HEADS):
        sl = slice(hd * MLA_HEAD_PAD, (hd + 1) * MLA_HEAD_PAD)
        qm_ref[:, sl] = (rope(q[:, sl]) * (MLA_QK_DIM ** -0.5 * LOG2_E)).astype(BF16)
        km_ref[:, sl] = (k_nope[:, sl] + k_rope).astype(BF16)


def _in_proj(x2d, pos, g_mix, w_in_p, g_cq, w_uq_p, g_ckv, w_uk_p, w_uv, invf, sgn, *, tm):
    t, d = x2d.shape
    row = lambda w: pl.BlockSpec((tm, w), lambda i: (i, 0))
    full = lambda a: pl.BlockSpec(a.shape, lambda i: (0, 0))
    outs = [(SB_WIDTH, BF16)] * 3 + [(MLA_HEADS * MLA_HEAD_PAD, BF16)] * 2 + [(MLA_WIDTH, BF16)]
    return pl.pallas_call(
        _in_proj_body,
        grid=(t // tm,),
        in_specs=[row(d), row(1), full(g_mix), full(w_in_p), full(g_cq), full(w_uq_p), full(g_ckv),
                  full(w_uk_p), full(w_uv), full(invf), full(sgn)],
        out_specs=[row(w) for w, _ in outs],
        out_shape=[jax.ShapeDtypeStruct((t, w), dt) for w, dt in outs],
        compiler_params=pltpu.CompilerParams(dimension_semantics=("arbitrary",),
                                             vmem_limit_bytes=VMEM_LIMIT_BYTES),
        name="in_proj",
    )(x2d, pos, g_mix, w_in_p, g_cq, w_uq_p, g_ckv, w_uk_p, w_uv, invf, sgn)


def _sb_body(q_ref, k_ref, v_ref, nsuf_ref, o_ref, qh_ref, tails_ref, acc_ref, *, tq):
    i = pl.program_id(1)
    n_pairs = SB_HEADS // 2
    lane = lax.broadcasted_iota(jnp.int32, (1, LANES), 1)
    lo = lane < SB_HEAD_DIM
    nsuf = nsuf_ref[...]
    visible = (lax.broadcasted_iota(jnp.int32, (tq, tq), 1) < lax.broadcasted_iota(jnp.int32, (tq, tq), 0))

    for pair in range(n_pairs):
        q = q_ref[0, :, pair * LANES:(pair + 1) * LANES]
        zero_q = jnp.zeros_like(q)
        qh_ref[2 * pair] = jnp.where(lo, q, zero_q)
        qh_ref[2 * pair + 1] = jnp.where(lo, zero_q, q)

    def load_kv(j):
        k0 = pl.multiple_of(j * tq, tq)
        return [(k_ref[0, pl.ds(k0, tq), pair * LANES:(pair + 1) * LANES],
                 v_ref[0, pl.ds(k0, tq), pair * LANES:(pair + 1) * LANES]) for pair in range(n_pairs)]

    def sweep_tiles(kvs, state):
        first_diag = state is None
        tails, accs = ([None] * SB_HEADS, [None] * n_pairs) if first_diag else (list(state[0]), list(state[1]))

        def logits(it):
            kt, _ = it["kv"][it["h"] // 2]
            z = _dot_nt(qh_ref[it["h"]], kt)
            keep = jnp.maximum(z, 0.0) + jnp.log2(1.0 + jnp.exp2(-jnp.abs(z)))
            it["log_beta"] = z - keep
            it["keep"] = jnp.where(visible, keep, 0.0) if it["diag"] else keep

        def weights(it):
            h, keep = it["h"], it["keep"]
            tail = _dot(keep.astype(BF16), nsuf)
            total = jnp.broadcast_to(tail[:, 0:1] - keep[:, 0:1], (tq, LANES))
            if tails[h] is not None:
                tail = tail + jnp.tile(tails[h], (1, tq // LANES))
            a = jnp.exp2(it["log_beta"] + tail)
            it["a"] = (jnp.where(visible, a, 0.0) if it["diag"] else a).astype(BF16)
            tails[h] = total if tails[h] is None else tails[h] + total

        def values(it):
            pair, hd = divmod(it["h"], 2)
            _, vt = it["kv"][pair]
            zero_v = jnp.zeros_like(vt)
            v_h = jnp.where(lo, vt, zero_v) if hd == 0 else jnp.where(lo, zero_v, vt)
            pv = _dot(it["a"], v_h)
            accs[pair] = pv if accs[pair] is None else accs[pair] + pv

        items = [dict(h=h, kv=kv, diag=first_diag and n == 0) for n, kv in enumerate(kvs) for h in range(SB_HEADS)]
        _emit_skewed(items, (logits, weights, values))
        return tails, accs

    def save(state):
        tails, accs = state
        for h in range(SB_HEADS):
            tails_ref[h] = tails[h]
        for pair in range(n_pairs):
            acc_ref[pair] = accs[pair]

    @pl.when(i == 0)
    def _():
        save(sweep_tiles([load_kv(0)], None))

    @pl.when(i > 0)
    def _():
        save(sweep_tiles([load_kv(i), load_kv(i - 1)], None))

    def alive():
        return jnp.max(tails_ref[...]) > SB_DEAD_LOG2

    def sweep(carry):
        j, _ = carry
        kv = load_kv(j)
        state = ([tails_ref[h] for h in range(SB_HEADS)], [acc_ref[pair] for pair in range(n_pairs)])
        save(sweep_tiles([kv], state))
        return j - 1, alive()

    @pl.when(i > 1)
    def _():
        lax.while_loop(lambda c: (c[0] >= 0) & c[1], sweep, (i - 2, alive()))

    for pair in range(n_pairs):
        o_ref[0, :, pair * LANES:(pair + 1) * LANES] = acc_ref[pair]


def _sb_attention(q, k, v, *, tq):
    b, s, w = q.shape
    idx = np.arange(tq)
    nsuf = jnp.asarray(np.where(idx[:, None] > idx[None, :], -1.0, 0.0), BF16)
    return pl.pallas_call(
        functools.partial(_sb_body, tq=tq),
        grid=(b, s // tq),
        in_specs=[pl.BlockSpec((1, tq, w), lambda bi, i: (bi, i, 0)),
                  pl.BlockSpec((1, s, w), lambda bi, i: (bi, 0, 0)),
                  pl.BlockSpec((1, s, w), lambda bi, i: (bi, 0, 0)),
                  pl.BlockSpec((tq, tq), lambda bi, i: (0, 0))],
        out_specs=pl.BlockSpec((1, tq, w), lambda bi, i: (bi, i, 0)),
        out_shape=jax.ShapeDtypeStruct((b, s, w), F32),
        scratch_shapes=[pltpu.VMEM((SB_HEADS, tq, LANES), BF16), pltpu.VMEM((SB_HEADS, tq, LANES), F32),
                        pltpu.VMEM((SB_HEADS // 2, tq, LANES), F32)],
        compiler_params=pltpu.CompilerParams(dimension_semantics=("arbitrary",) * 2,
                                             vmem_limit_bytes=VMEM_LIMIT_BYTES),
        name="sb_attention",
    )(q, k, v, nsuf)


def _mla_body(q_ref, k_ref, v_ref, o_ref, *, tq, n_q):
    i = pl.program_id(1)
    lane = lax.broadcasted_iota(jnp.int32, (1, LANES), 1)
    lo = lane < MLA_V_DIM
    visible = (lax.broadcasted_iota(jnp.int32, (tq, tq), 1) <= lax.broadcasted_iota(jnp.int32, (tq, tq), 0))

    def run(n_tiles):
        m = [None] * MLA_HEADS
        acc = [None] * MLA_HEADS

        def probs(it):
            hd, rows = it["hd"], it["rows"]
            q = q_ref[0, :, hd * MLA_HEAD_PAD:(hd + 1) * MLA_HEAD_PAD]
            kt = k_ref[0, rows, hd * MLA_HEAD_PAD:(hd + 1) * MLA_HEAD_PAD]
            s = _dot_nt(q, kt)
            if m[hd] is None:
                s = jnp.where(visible, s, -jnp.inf)
                m_new = jnp.broadcast_to(jnp.max(s, axis=-1, keepdims=True), (tq, LANES))
                it["alpha"] = None
            else:
                m_new = jnp.maximum(m[hd], jnp.max(s, axis=-1, keepdims=True))
                it["alpha"] = jnp.exp2(m[hd] - m_new)
            it["p"] = jnp.exp2(s - jnp.tile(m_new, (1, tq // LANES))).astype(BF16)
            m[hd] = m_new

        def values(it):
            hd = it["hd"]
            vt = v_ref[0, it["rows"], (hd // 2) * LANES:(hd // 2 + 1) * LANES]
            one = jnp.ones_like(vt)
            v_ext = jnp.where(lo, vt, one) if hd % 2 == 0 else jnp.where(lo, one, vt)
            pv = _dot(it["p"], v_ext)
            acc[hd] = pv if it["alpha"] is None else it["alpha"] * acc[hd] + pv

        order = [n_tiles - 1] + list(range(n_tiles - 1))
        items = [dict(hd=hd, rows=slice(j * tq, (j + 1) * tq)) for j in order for hd in range(MLA_HEADS)]
        _emit_skewed(items, (probs, values))
        for pair in range(MLA_HEADS // 2):
            a0, a1 = acc[2 * pair], acc[2 * pair + 1]
            o_ref[0, :, pair * LANES:(pair + 1) * LANES] = jnp.where(
                lo, a0 / pltpu.roll(a0, MLA_V_DIM, 1), a1 / pltpu.roll(a1, MLA_V_DIM, 1))

    for c in range(n_q):
        pl.when(i == c)(functools.partial(run, c + 1))


def _mla_attention(q, k, v, *, tq):
    b, s, wq = q.shape
    w = v.shape[-1]
    return pl.pallas_call(
        functools.partial(_mla_body, tq=tq, n_q=s // tq),
        grid=(b, s // tq),
        in_specs=[pl.BlockSpec((1, tq, wq), lambda bi, i: (bi, i, 0)),
                  pl.BlockSpec((1, s, wq), lambda bi, i: (bi, 0, 0)),
                  pl.BlockSpec((1, s, w), lambda bi, i: (bi, 0, 0))],
        out_specs=pl.BlockSpec((1, tq, w), lambda bi, i: (bi, i, 0)),
        out_shape=jax.ShapeDtypeStruct((b, s, w), F32),
        compiler_params=pltpu.CompilerParams(dimension_semantics=("arbitrary",) * 2,
                                             vmem_limit_bytes=VMEM_LIMIT_BYTES),
        name="mla_attention",
    )(q, k, v)


def _ffn_body(x_ref, osb_ref, omla_ref, g_sb_ref, g_mla_ref, w_out_ref, g_ffn_ref, w_up_ref, conv_ref,
              w_down_ref, g_final_ref, out_ref, h_ref, acc_ref, u_ref, *, tm, ck, tiles_per_seq):
    step = pl.program_id(0)
    halo = BF16_SUBLANES
    d = x_ref.shape[-1]

    @pl.when(step % tiles_per_seq == 0)
    def _():
        h_ref[0:halo, :] = jnp.zeros((halo, d), BF16)

    @pl.when(step % tiles_per_seq != 0)
    def _():
        h_ref[0:halo, :] = h_ref[tm:tm + halo, :]

    o = jnp.concatenate([_rms(osb_ref[...], g_sb_ref[...]), _rms(omla_ref[...], g_mla_ref[...])], axis=-1)
    x2 = x_ref[...] + _dot(o.astype(BF16), w_out_ref[...])
    acc_ref[...] = x2
    h_ref[halo:, :] = _rms(x2, g_ffn_ref[...]).astype(BF16)

    def up(c, slot):
        u_ref[slot] = _dot(h_ref[...], w_up_ref[c])

    def down(c, slot):
        u = u_ref[slot]
        cw = conv_ref[c]
        y = u[halo:] * cw[CONV_WIDTH - 1:CONV_WIDTH] + cw[CONV_WIDTH:CONV_WIDTH + 1]
        for back in range(1, CONV_WIDTH):
            tap = cw[CONV_WIDTH - 1 - back:CONV_WIDTH - back]
            y = y + pltpu.roll(u, back, 0)[halo:] * tap
        gate, val = y[:, :ck], y[:, ck:]
        act = gate * (1.0 / (1.0 + jnp.exp(-gate))) * val
        acc_ref[...] += _dot(act.astype(BF16), w_down_ref[c])

    n_chunks = w_up_ref.shape[0]
    up(0, 0)

    def two_chunks(p, _):
        up(2 * p + 1, 1)
        down(2 * p, 0)
        up(2 * p + 2, 0)
        down(2 * p + 1, 1)
        return 0

    lax.fori_loop(0, (n_chunks - 1) // 2, two_chunks, 0)
    down(n_chunks - 1, 0)
    out_ref[...] = _rms(acc_ref[...], g_final_ref[...])


def _ffn(x2d, osb, omla, g_sb, g_mla, w_out, g_ffn, w_up_c, conv_c, w_down_c, g_final, *, tm, seq):
    t, d = x2d.shape
    n_chunks, _, ck2 = w_up_c.shape
    row = lambda w: pl.BlockSpec((tm, w), lambda i: (i, 0))
    const = lambda a: pl.BlockSpec(a.shape, lambda i: (0,) * a.ndim, pipeline_mode=pl.Buffered(1))
    return pl.pallas_call(
        functools.partial(_ffn_body, tm=tm, ck=ck2 // 2, tiles_per_seq=seq // tm),
        grid=(t // tm,),
        in_specs=[row(d), row(SB_WIDTH), row(MLA_WIDTH), const(g_sb), const(g_mla), const(w_out),
                  const(g_ffn), const(w_up_c), const(conv_c), const(w_down_c), const(g_final)],
        out_specs=row(d),
        out_shape=jax.ShapeDtypeStruct((t, d), F32),
        scratch_shapes=[pltpu.VMEM((tm + BF16_SUBLANES, d), BF16), pltpu.VMEM((tm, d), F32),
                        pltpu.VMEM((2, tm + BF16_SUBLANES, ck2), F32)],
        compiler_params=pltpu.CompilerParams(dimension_semantics=("arbitrary",),
                                             vmem_limit_bytes=VMEM_LIMIT_BYTES),
        name="out_proj_ffn",
    )(x2d, osb, omla, g_sb, g_mla, w_out, g_ffn, w_up_c, conv_c, w_down_c, g_final)


def _tiles(seq):
    return dict(tm_in=512, tq=256, tm_ffn=512, ck=256)


def kernel(x, positions, g_mix, w_in, g_cq, w_uq, g_ckv, w_ukv, g_sb_out, g_mla_out, w_out, g_ffn,
           w_up, conv_w, conv_b, w_down, g_final):
    b, s, d = x.shape
    depth = w_in.shape[0]
    t = b * s
    cfg = _tiles(s)
    assert s % cfg["tm_ffn"] == 0 and s % cfg["tq"] == 0 and t % cfg["tm_in"] == 0
    assert D_FF % cfg["ck"] == 0 and (D_FF // cfg["ck"]) % 2 == 1

    inv_freq = 1.0 / (ROPE_BASE ** (jnp.arange(ROPE_HALF, dtype=F32) * (2.0 / MLA_ROPE_DIM)))
    pad_l, pad_r = ROPE_LANE0, LANES - ROPE_LANE0 - MLA_ROPE_DIM
    invf = jnp.pad(jnp.tile(inv_freq, 2), (pad_l, pad_r))[None, :]
    sgn = jnp.pad(jnp.concatenate([-jnp.ones(ROPE_HALF, F32), jnp.ones(ROPE_HALF, F32)]), (pad_l, pad_r))[None, :]
    pos = positions.astype(F32).reshape(t, 1)

    x2d = x.reshape(t, d)
    n_chunks = D_FF // cfg["ck"]
    for l in range(depth):
        n_main = 3 * SB_WIDTH + Q_LORA_RANK + KV_LORA_RANK
        w_in_p = jnp.concatenate(
            [w_in[l][:, :n_main], jnp.pad(w_in[l][:, n_main:], ((0, 0), (pad_l, pad_r)))], axis=1).astype(BF16)
        w_uq_p = jnp.pad(w_uq[l].reshape(Q_LORA_RANK, MLA_HEADS, MLA_QK_DIM),
                         ((0, 0), (0, 0), (0, MLA_HEAD_PAD - MLA_QK_DIM))
                         ).reshape(Q_LORA_RANK, MLA_HEADS * MLA_HEAD_PAD).astype(BF16)
        w_ukv_h = w_ukv[l].reshape(KV_LORA_RANK, MLA_HEADS, MLA_NOPE_DIM + MLA_V_DIM)
        w_uk_p = jnp.pad(w_ukv_h[:, :, :MLA_NOPE_DIM], ((0, 0), (0, 0), (0, MLA_HEAD_PAD - MLA_NOPE_DIM))
                         ).reshape(KV_LORA_RANK, MLA_HEADS * MLA_HEAD_PAD).astype(BF16)
        w_uv = w_ukv_h[:, :, MLA_NOPE_DIM:].reshape(KV_LORA_RANK, MLA_WIDTH).astype(BF16)

        qsb, ksb, vsb, qm, km, vm = _in_proj(
            x2d, pos, g_mix[l][None, :], w_in_p, g_cq[l][None, :], w_uq_p, g_ckv[l][None, :], w_uk_p, w_uv,
            invf, sgn, tm=cfg["tm_in"])

        o_sb = _sb_attention(qsb.reshape(b, s, -1), ksb.reshape(b, s, -1), vsb.reshape(b, s, -1), tq=cfg["tq"])
        o_mla = _mla_attention(qm.reshape(b, s, -1), km.reshape(b, s, -1), vm.reshape(b, s, -1), tq=cfg["tq"])

        ck = cfg["ck"]
        w_up_c = jnp.concatenate([w_up[l][:, :D_FF].reshape(d, n_chunks, ck),
                                  w_up[l][:, D_FF:].reshape(d, n_chunks, ck)], axis=-1
                                 ).transpose(1, 0, 2).astype(BF16)
        conv_rows = jnp.concatenate([conv_w[l], conv_b[l][None, :]], axis=0)
        conv_c = jnp.concatenate([conv_rows[:, :D_FF].reshape(-1, n_chunks, ck),
                                  conv_rows[:, D_FF:].reshape(-1, n_chunks, ck)], axis=-1).transpose(1, 0, 2)
        conv_c = jnp.pad(conv_c, ((0, 0), (0, 8 - conv_c.shape[1]), (0, 0)))
        w_down_c = w_down[l].reshape(n_chunks, ck, d).astype(BF16)
        last = l == depth - 1
        g_last = g_final[None, :] if last else None
        assert last, "the fused final rmsnorm assumes a single layer"
        x2d = _ffn(x2d, o_sb.reshape(t, -1), o_mla.reshape(t, -1), g_sb_out[l][None, :], g_mla_out[l][None, :],
                   w_out[l].astype(BF16), g_ffn[l][None, :], w_up_c, conv_c, w_down_c, g_last,
                   tm=cfg["tm_ffn"], seq=s)
    return x2d.reshape(b, s, d)
```

```python
import functools

import numpy as np
import jax
import jax.numpy as jnp
from jax import lax
from jax.experimental import pallas as pl
from jax.experimental.pallas import tpu as pltpu

SB_HEADS = 8
SB_HEAD_DIM = 64
MLA_HEADS = 8
MLA_NOPE_DIM = 64
MLA_ROPE_DIM = 32
MLA_V_DIM = 64
Q_LORA_RANK = 384
KV_LORA_RANK = 256
D_FF = 2816
CONV_WIDTH = 3
ROPE_BASE = 10000.0
EPS = 1e-6

SB_WIDTH = SB_HEADS * SB_HEAD_DIM
MLA_WIDTH = MLA_HEADS * MLA_V_DIM
MLA_QK_DIM = MLA_NOPE_DIM + MLA_ROPE_DIM

LANES = 128
BF16_SUBLANES = 16
VMEM_LIMIT_BYTES = 56 * 1024 * 1024

MLA_HEAD_PAD = LANES
ROPE_LANE0 = MLA_NOPE_DIM
ROPE_HALF = MLA_ROPE_DIM // 2

BF16 = jnp.bfloat16
F32 = jnp.float32

LOG2_E = 1.4426950408889634
SB_DEAD_LOG2 = -152.0
FFN_GROUP = 6


def _rms(x, g):
    return x * lax.rsqrt(jnp.mean(x * x, axis=-1, keepdims=True) + EPS) * g


def _dot(a, b):
    return jnp.dot(a, b, preferred_element_type=F32)


def _dot_nt(a, b):
    return lax.dot_general(a, b, (((1,), (1,)), ((), ())), preferred_element_type=F32)


def _emit_skewed(items, stages, skew=1):
    for step in range(len(items) + (len(stages) - 1) * skew):
        for k, stage in enumerate(stages):
            n = step - k * skew
            if 0 <= n < len(items):
                stage(items[n])


def _in_proj_body(x_ref, pos_ref, g_mix_ref, w_in_ref, g_cq_ref, w_uq_ref, g_ckv_ref, w_uk_ref,
                  w_uv_ref, invf_ref, sgn_ref,
                  qsb_ref, ksb_ref, vsb_ref, qm_ref, km_ref, vm_ref):
    h = _rms(x_ref[...], g_mix_ref[...]).astype(BF16)

    c0 = 3 * SB_WIDTH
    c_q = _dot(h, w_in_ref[:, c0:c0 + Q_LORA_RANK])
    c0 += Q_LORA_RANK
    c_kv = _dot(h, w_in_ref[:, c0:c0 + KV_LORA_RANK])
    c0 += KV_LORA_RANK
    k_rope = _dot(h, w_in_ref[:, c0:c0 + LANES])
    qsb_ref[...] = (_dot(h, w_in_ref[:, 0:SB_WIDTH]) * (SB_HEAD_DIM ** -0.5 * LOG2_E)).astype(BF16)
    ksb_ref[...] = _dot(h, w_in_ref[:, SB_WIDTH:2 * SB_WIDTH]).astype(BF16)
    vsb_ref[...] = _dot(h, w_in_ref[:, 2 * SB_WIDTH:3 * SB_WIDTH]).astype(BF16)

    ang = pos_ref[...] * invf_ref[...]
    cos_t = jnp.cos(ang)
    sin_t = jnp.sin(ang) * sgn_ref[...]
    lane = lax.broadcasted_iota(jnp.int32, (1, LANES), 1)
    first_half = lane < ROPE_LANE0 + ROPE_HALF

    def rope(v):
        partner = jnp.where(first_half, pltpu.roll(v, LANES - ROPE_HALF, 1), pltpu.roll(v, ROPE_HALF, 1))
        return v * cos_t + partner * sin_t

    cqn = _rms(c_q, g_cq_ref[...]).astype(BF16)
    q = _dot(cqn, w_uq_ref[...])
    ckvn = _rms(c_kv, g_ckv_ref[...]).astype(BF16)
    k_nope = _dot(ckvn, w_uk_ref[...])
    vm_ref[...] = _dot(ckvn, w_uv_ref[...]).astype(BF16)
    k_rope = rope(k_rope)
    for hd in range(MLA_HEADS):
        sl = slice(hd * MLA_HEAD_PAD, (hd + 1) * MLA_HEAD_PAD)
        qm_ref[:, sl] = (rope(q[:, sl]) * (MLA_QK_DIM ** -0.5 * LOG2_E)).astype(BF16)
        km_ref[:, sl] = (k_nope[:, sl] + k_rope).astype(BF16)


def _in_proj(x2d, pos, g_mix, w_in_p, g_cq, w_uq_p, g_ckv, w_uk_p, w_uv, invf, sgn, *, tm):
    t, d = x2d.shape
    row = lambda w: pl.BlockSpec((tm, w), lambda i: (i, 0))
    full = lambda a: pl.BlockSpec(a.shape, lambda i: (0, 0))
    outs = [(SB_WIDTH, BF16)] * 3 + [(MLA_HEADS * MLA_HEAD_PAD, BF16)] * 2 + [(MLA_WIDTH, BF16)]
    return pl.pallas_call(
        _in_proj_body,
        grid=(t // tm,),
        in_specs=[row(d), row(1), full(g_mix), full(w_in_p), full(g_cq), full(w_uq_p), full(g_ckv),
                  full(w_uk_p), full(w_uv), full(invf), full(sgn)],
        out_specs=[row(w) for w, _ in outs],
        out_shape=[jax.ShapeDtypeStruct((t, w), dt) for w, dt in outs],
        compiler_params=pltpu.CompilerParams(dimension_semantics=("arbitrary",),
                                             vmem_limit_bytes=VMEM_LIMIT_BYTES),
        name="in_proj",
    )(x2d, pos, g_mix, w_in_p, g_cq, w_uq_p, g_ckv, w_uk_p, w_uv, invf, sgn)


def _sb_body(q_ref, k_ref, v_ref, nsuf_ref, o_ref, qh_ref, tails_ref, acc_ref, *, tq):
    i = pl.program_id(1)
    n_pairs = SB_HEADS // 2
    lane = lax.broadcasted_iota(jnp.int32, (1, LANES), 1)
    lo = lane < SB_HEAD_DIM
    nsuf = nsuf_ref[...]
    visible = (lax.broadcasted_iota(jnp.int32, (tq, tq), 1) < lax.broadcasted_iota(jnp.int32, (tq, tq), 0))

    for pair in range(n_pairs):
        q = q_ref[0, :, pair * LANES:(pair + 1) * LANES]
        zero_q = jnp.zeros_like(q)
        qh_ref[2 * pair] = jnp.where(lo, q, zero_q)
        qh_ref[2 * pair + 1] = jnp.where(lo, zero_q, q)

    def load_kv(j):
        k0 = pl.multiple_of(j * tq, tq)
        return [(k_ref[0, pl.ds(k0, tq), pair * LANES:(pair + 1) * LANES],
                 v_ref[0, pl.ds(k0, tq), pair * LANES:(pair + 1) * LANES]) for pair in range(n_pairs)]

    def sweep_tiles(kvs, state):
        first_diag = state is None
        tails, accs = ([None] * SB_HEADS, [None] * n_pairs) if first_diag else (list(state[0]), list(state[1]))

        def logits(it):
            kt, _ = it["kv"][it["h"] // 2]
            z = _dot_nt(qh_ref[it["h"]], kt)
            keep = jnp.maximum(z, 0.0) + jnp.log2(1.0 + jnp.exp2(-jnp.abs(z)))
            it["log_beta"] = z - keep
            it["keep"] = jnp.where(visible, keep, 0.0) if it["diag"] else keep

        def weights(it):
            h, keep = it["h"], it["keep"]
            tail = _dot(keep.astype(BF16), nsuf)
            total = jnp.broadcast_to(tail[:, 0:1] - keep[:, 0:1], (tq, LANES))
            if tails[h] is not None:
                tail = tail + jnp.tile(tails[h], (1, tq // LANES))
            a = jnp.exp2(it["log_beta"] + tail)
            it["a"] = (jnp.where(visible, a, 0.0) if it["diag"] else a).astype(BF16)
            tails[h] = total if tails[h] is None else tails[h] + total

        def values(it):
            pair, hd = divmod(it["h"], 2)
            _, vt = it["kv"][pair]
            zero_v = jnp.zeros_like(vt)
            v_h = jnp.where(lo, vt, zero_v) if hd == 0 else jnp.where(lo, zero_v, vt)
            pv = _dot(it["a"], v_h)
            accs[pair] = pv if accs[pair] is None else accs[pair] + pv

        items = [dict(h=h, kv=kv, diag=first_diag and n == 0) for n, kv in enumerate(kvs) for h in range(SB_HEADS)]
        _emit_skewed(items, (logits, weights, values))
        return tails, accs

    def save(state):
        tails, accs = state
        for h in range(SB_HEADS):
            tails_ref[h] = tails[h]
        for pair in range(n_pairs):
            acc_ref[pair] = accs[pair]

    @pl.when(i == 0)
    def _():
        save(sweep_tiles([load_kv(0)], None))

    @pl.when(i > 0)
    def _():
        save(sweep_tiles([load_kv(i), load_kv(i - 1)], None))

    def alive():
        return jnp.max(tails_ref[...]) > SB_DEAD_LOG2

    def sweep(carry):
        j, _ = carry
        kv = load_kv(j)
        state = ([tails_ref[h] for h in range(SB_HEADS)], [acc_ref[pair] for pair in range(n_pairs)])
        save(sweep_tiles([kv], state))
        return j - 1, alive()

    @pl.when(i > 1)
    def _():
        lax.while_loop(lambda c: (c[0] >= 0) & c[1], sweep, (i - 2, alive()))

    for pair in range(n_pairs):
        o_ref[0, :, pair * LANES:(pair + 1) * LANES] = acc_ref[pair]


def _sb_attention(q, k, v, *, tq):
    b, s, w = q.shape
    idx = np.arange(tq)
    nsuf = jnp.asarray(np.where(idx[:, None] > idx[None, :], -1.0, 0.0), BF16)
    return pl.pallas_call(
        functools.partial(_sb_body, tq=tq),
        grid=(b, s // tq),
        in_specs=[pl.BlockSpec((1, tq, w), lambda bi, i: (bi, i, 0)),
                  pl.BlockSpec((1, s, w), lambda bi, i: (bi, 0, 0)),
                  pl.BlockSpec((1, s, w), lambda bi, i: (bi, 0, 0)),
                  pl.BlockSpec((tq, tq), lambda bi, i: (0, 0))],
        out_specs=pl.BlockSpec((1, tq, w), lambda bi, i: (bi, i, 0)),
        out_shape=jax.ShapeDtypeStruct((b, s, w), F32),
        scratch_shapes=[pltpu.VMEM((SB_HEADS, tq, LANES), BF16), pltpu.VMEM((SB_HEADS, tq, LANES), F32),
                        pltpu.VMEM((SB_HEADS // 2, tq, LANES), F32)],
        compiler_params=pltpu.CompilerParams(dimension_semantics=("arbitrary",) * 2,
                                             vmem_limit_bytes=VMEM_LIMIT_BYTES),
        name="sb_attention",
    )(q, k, v, nsuf)


def _mla_body(q_ref, k_ref, v_ref, o_ref, *, tq, n_q):
    i = pl.program_id(1)
    lane = lax.broadcasted_iota(jnp.int32, (1, LANES), 1)
    lo = lane < MLA_V_DIM
    visible = (lax.broadcasted_iota(jnp.int32, (tq, tq), 1) <= lax.broadcasted_iota(jnp.int32, (tq, tq), 0))

    def run(n_tiles):
        m = [None] * MLA_HEADS
        acc = [None] * MLA_HEADS

        def probs(it):
            hd, rows = it["hd"], it["rows"]
            q = q_ref[0, :, hd * MLA_HEAD_PAD:(hd + 1) * MLA_HEAD_PAD]
            kt = k_ref[0, rows, hd * MLA_HEAD_PAD:(hd + 1) * MLA_HEAD_PAD]
            s = _dot_nt(q, kt)
            if m[hd] is None:
                s = jnp.where(visible, s, -jnp.inf)
                m_new = jnp.broadcast_to(jnp.max(s, axis=-1, keepdims=True), (tq, LANES))
                it["alpha"] = None
            else:
                m_new = jnp.maximum(m[hd], jnp.max(s, axis=-1, keepdims=True))
                it["alpha"] = jnp.exp2(m[hd] - m_new)
            it["p"] = jnp.exp2(s - jnp.tile(m_new, (1, tq // LANES))).astype(BF16)
            m[hd] = m_new

        def values(it):
            hd = it["hd"]
            vt = v_ref[0, it["rows"], (hd // 2) * LANES:(hd // 2 + 1) * LANES]
            one = jnp.ones_like(vt)
            v_ext = jnp.where(lo, vt, one) if hd % 2 == 0 else jnp.where(lo, one, vt)
            pv = _dot(it["p"], v_ext)
            acc[hd] = pv if it["alpha"] is None else it["alpha"] * acc[hd] + pv

        order = [n_tiles - 1] + list(range(n_tiles - 1))
        items = [dict(hd=hd, rows=slice(j * tq, (j + 1) * tq)) for j in order for hd in range(MLA_HEADS)]
        _emit_skewed(items, (probs, values), skew=MLA_HEADS)
        for pair in range(MLA_HEADS // 2):
            a0, a1 = acc[2 * pair], acc[2 * pair + 1]
            o_ref[0, :, pair * LANES:(pair + 1) * LANES] = jnp.where(
                lo, a0 / pltpu.roll(a0, MLA_V_DIM, 1), a1 / pltpu.roll(a1, MLA_V_DIM, 1))

    for c in range(n_q):
        pl.when(i == c)(functools.partial(run, c + 1))


def _mla_attention(q, k, v, *, tq):
    b, s, wq = q.shape
    w = v.shape[-1]
    return pl.pallas_call(
        functools.partial(_mla_body, tq=tq, n_q=s // tq),
        grid=(b, s // tq),
        in_specs=[pl.BlockSpec((1, tq, wq), lambda bi, i: (bi, i, 0)),
                  pl.BlockSpec((1, s, wq), lambda bi, i: (bi, 0, 0)),
                  pl.BlockSpec((1, s, w), lambda bi, i: (bi, 0, 0))],
        out_specs=pl.BlockSpec((1, tq, w), lambda bi, i: (bi, i, 0)),
        out_shape=jax.ShapeDtypeStruct((b, s, w), F32),
        compiler_params=pltpu.CompilerParams(dimension_semantics=("arbitrary",) * 2,
                                             vmem_limit_bytes=VMEM_LIMIT_BYTES),
        name="mla_attention",
    )(q, k, v)


def _ffn_body(x_ref, osb_ref, omla_ref, g_sb_ref, g_mla_ref, w_out_ref, g_ffn_ref, w_up_ref, conv_ref,
              w_down_ref, g_final_ref, out_ref, h_ref, acc_ref, u_ref, *, tm, ck, tiles_per_seq):
    step = pl.program_id(0)
    halo = BF16_SUBLANES
    d = x_ref.shape[-1]

    @pl.when(step % tiles_per_seq == 0)
    def _():
        h_ref[0:halo, :] = jnp.zeros((halo, d), BF16)

    @pl.when(step % tiles_per_seq != 0)
    def _():
        h_ref[0:halo, :] = h_ref[tm:tm + halo, :]

    o = jnp.concatenate([_rms(osb_ref[...], g_sb_ref[...]), _rms(omla_ref[...], g_mla_ref[...])], axis=-1)
    x2 = x_ref[...] + _dot(o.astype(BF16), w_out_ref[...])
    acc_ref[...] = x2
    h_ref[halo:, :] = _rms(x2, g_ffn_ref[...]).astype(BF16)

    def up(c, slot):
        u_ref[slot] = _dot(h_ref[...], w_up_ref[c])

    def down(c, slot):
        u = u_ref[slot]
        cw = conv_ref[c]
        y = u[halo:] * cw[CONV_WIDTH - 1:CONV_WIDTH] + cw[CONV_WIDTH:CONV_WIDTH + 1]
        for back in range(1, CONV_WIDTH):
            tap = cw[CONV_WIDTH - 1 - back:CONV_WIDTH - back]
            y = y + pltpu.roll(u, back, 0)[halo:] * tap
        gate, val = y[:, :ck], y[:, ck:]
        act = gate * (1.0 / (1.0 + jnp.exp(-gate))) * val
        acc_ref[...] += _dot(act.astype(BF16), w_down_ref[c])

    def group(c0, size):
        for k in range(size):
            up(c0 + k, k)
        for k in range(size):
            down(c0 + k, k)

    n_chunks = w_up_ref.shape[0]
    n_groups, rest = divmod(n_chunks, FFN_GROUP)

    def one_group(g, _):
        group(g * FFN_GROUP, FFN_GROUP)
        return 0

    lax.fori_loop(0, n_groups, one_group, 0)
    if rest:
        group(n_groups * FFN_GROUP, rest)
    out_ref[...] = _rms(acc_ref[...], g_final_ref[...])


def _ffn(x2d, osb, omla, g_sb, g_mla, w_out, g_ffn, w_up_c, conv_c, w_down_c, g_final, *, tm, seq):
    t, d = x2d.shape
    n_chunks, _, ck2 = w_up_c.shape
    row = lambda w: pl.BlockSpec((tm, w), lambda i: (i, 0))
    const = lambda a: pl.BlockSpec(a.shape, lambda i: (0,) * a.ndim, pipeline_mode=pl.Buffered(1))
    return pl.pallas_call(
        functools.partial(_ffn_body, tm=tm, ck=ck2 // 2, tiles_per_seq=seq // tm),
        grid=(t // tm,),
        in_specs=[row(d), row(SB_WIDTH), row(MLA_WIDTH), const(g_sb), const(g_mla), const(w_out),
                  const(g_ffn), const(w_up_c), const(conv_c), const(w_down_c), const(g_final)],
        out_specs=row(d),
        out_shape=jax.ShapeDtypeStruct((t, d), F32),
        scratch_shapes=[pltpu.VMEM((tm + BF16_SUBLANES, d), BF16), pltpu.VMEM((tm, d), F32),
                        pltpu.VMEM((FFN_GROUP, tm + BF16_SUBLANES, ck2), F32)],
        compiler_params=pltpu.CompilerParams(dimension_semantics=("arbitrary",),
                                             vmem_limit_bytes=VMEM_LIMIT_BYTES),
        name="out_proj_ffn",
    )(x2d, osb, omla, g_sb, g_mla, w_out, g_ffn, w_up_c, conv_c, w_down_c, g_final)


def _tiles(seq):
    return dict(tm_in=512, tq=256, tm_ffn=512, ck=256)


def kernel(x, positions, g_mix, w_in, g_cq, w_uq, g_ckv, w_ukv, g_sb_out, g_mla_out, w_out, g_ffn,
           w_up, conv_w, conv_b, w_down, g_final):
    b, s, d = x.shape
    depth = w_in.shape[0]
    t = b * s
    cfg = _tiles(s)
    assert s % cfg["tm_ffn"] == 0 and s % cfg["tq"] == 0 and t % cfg["tm_in"] == 0
    assert D_FF % cfg["ck"] == 0

    inv_freq = 1.0 / (ROPE_BASE ** (jnp.arange(ROPE_HALF, dtype=F32) * (2.0 / MLA_ROPE_DIM)))
    pad_l, pad_r = ROPE_LANE0, LANES - ROPE_LANE0 - MLA_ROPE_DIM
    invf = jnp.pad(jnp.tile(inv_freq, 2), (pad_l, pad_r))[None, :]
    sgn = jnp.pad(jnp.concatenate([-jnp.ones(ROPE_HALF, F32), jnp.ones(ROPE_HALF, F32)]), (pad_l, pad_r))[None, :]
    pos = positions.astype(F32).reshape(t, 1)

    x2d = x.reshape(t, d)
    n_chunks = D_FF // cfg["ck"]
    for l in range(depth):
        n_main = 3 * SB_WIDTH + Q_LORA_RANK + KV_LORA_RANK
        w_in_p = jnp.concatenate(
            [w_in[l][:, :n_main], jnp.pad(w_in[l][:, n_main:], ((0, 0), (pad_l, pad_r)))], axis=1).astype(BF16)
        w_uq_p = jnp.pad(w_uq[l].reshape(Q_LORA_RANK, MLA_HEADS, MLA_QK_DIM),
                         ((0, 0), (0, 0), (0, MLA_HEAD_PAD - MLA_QK_DIM))
                         ).reshape(Q_LORA_RANK, MLA_HEADS * MLA_HEAD_PAD).astype(BF16)
        w_ukv_h = w_ukv[l].reshape(KV_LORA_RANK, MLA_HEADS, MLA_NOPE_DIM + MLA_V_DIM)
        w_uk_p = jnp.pad(w_ukv_h[:, :, :MLA_NOPE_DIM], ((0, 0), (0, 0), (0, MLA_HEAD_PAD - MLA_NOPE_DIM))
                         ).reshape(KV_LORA_RANK, MLA_HEADS * MLA_HEAD_PAD).astype(BF16)
        w_uv = w_ukv_h[:, :, MLA_NOPE_DIM:].reshape(KV_LORA_RANK, MLA_WIDTH).astype(BF16)

        qsb, ksb, vsb, qm, km, vm = _in_proj(
            x2d, pos, g_mix[l][None, :], w_in_p, g_cq[l][None, :], w_uq_p, g_ckv[l][None, :], w_uk_p, w_uv,
            invf, sgn, tm=cfg["tm_in"])

        o_sb = _sb_attention(qsb.reshape(b, s, -1), ksb.reshape(b, s, -1), vsb.reshape(b, s, -1), tq=cfg["tq"])
        o_mla = _mla_attention(qm.reshape(b, s, -1), km.reshape(b, s, -1), vm.reshape(b, s, -1), tq=cfg["tq"])

        ck = cfg["ck"]
        w_up_c = jnp.concatenate([w_up[l][:, :D_FF].reshape(d, n_chunks, ck),
                                  w_up[l][:, D_FF:].reshape(d, n_chunks, ck)], axis=-1
                                 ).transpose(1, 0, 2).astype(BF16)
        conv_rows = jnp.concatenate([conv_w[l], conv_b[l][None, :]], axis=0)
        conv_c = jnp.concatenate([conv_rows[:, :D_FF].reshape(-1, n_chunks, ck),
                                  conv_rows[:, D_FF:].reshape(-1, n_chunks, ck)], axis=-1).transpose(1, 0, 2)
        conv_c = jnp.pad(conv_c, ((0, 0), (0, 8 - conv_c.shape[1]), (0, 0)))
        w_down_c = w_down[l].reshape(n_chunks, ck, d).astype(BF16)
        last = l == depth - 1
        g_last = g_final[None, :] if last else None
        assert last, "the fused final rmsnorm assumes a single layer"
        x2d = _ffn(x2d, o_sb.reshape(t, -1), o_mla.reshape(t, -1), g_sb_out[l][None, :], g_mla_out[l][None, :],
                   w_out[l].astype(BF16), g_ffn[l][None, :], w_up_c, conv_c, w_down_c, g_last,
                   tm=cfg["tm_ffn"], seq=s)
    return x2d.reshape(b, s, d)
```

```python
import functools

import numpy as np
import jax
import jax.numpy as jnp
from jax import lax
from jax.experimental import pallas as pl
from jax.experimental.pallas import tpu as pltpu

SB_HEADS = 8
SB_HEAD_DIM = 64
MLA_HEADS = 8
MLA_NOPE_DIM = 64
MLA_ROPE_DIM = 32
MLA_V_DIM = 64
Q_LORA_RANK = 384
KV_LORA_RANK = 256
D_FF = 2816
CONV_WIDTH = 3
ROPE_BASE = 10000.0
EPS = 1e-6

SB_WIDTH = SB_HEADS * SB_HEAD_DIM
MLA_WIDTH = MLA_HEADS * MLA_V_DIM
MLA_QK_DIM = MLA_NOPE_DIM + MLA_ROPE_DIM

LANES = 128
SUBLANES = 8
BF16_SUBLANES = 16
VMEM_LIMIT_BYTES = 56 * 1024 * 1024

MLA_HEAD_PAD = LANES
ROPE_LANE0 = MLA_NOPE_DIM
ROPE_HALF = MLA_ROPE_DIM // 2

BF16 = jnp.bfloat16
F32 = jnp.float32

LOG2_E = 1.4426950408889634
SB_DEAD_LOG2 = -152.0
FFN_GROUP = 6


def _rms(x, g):
    return x * lax.rsqrt(jnp.mean(x * x, axis=-1, keepdims=True) + EPS) * g


def _dot(a, b):
    return jnp.dot(a, b, preferred_element_type=F32)


def _dot_nt(a, b):
    return lax.dot_general(a, b, (((1,), (1,)), ((), ())), preferred_element_type=F32)


def _emit_skewed(items, stages, skew=1):
    for step in range(len(items) + (len(stages) - 1) * skew):
        for k, stage in enumerate(stages):
            n = step - k * skew
            if 0 <= n < len(items):
                stage(items[n])


def _in_proj_body(x_ref, pos_ref, g_mix_ref, w_in_ref, g_cq_ref, w_uq_ref, g_ckv_ref, w_uk_ref,
                  w_uv_ref, invf_ref, sgn_ref,
                  qsb_ref, ksb_ref, vsb_ref, qm_ref, km_ref, vm_ref):
    h = _rms(x_ref[...], g_mix_ref[...]).astype(BF16)

    c0 = 3 * SB_WIDTH
    c_q = _dot(h, w_in_ref[:, c0:c0 + Q_LORA_RANK])
    c0 += Q_LORA_RANK
    c_kv = _dot(h, w_in_ref[:, c0:c0 + KV_LORA_RANK])
    c0 += KV_LORA_RANK
    k_rope = _dot(h, w_in_ref[:, c0:c0 + LANES])
    qsb_ref[...] = (_dot(h, w_in_ref[:, 0:SB_WIDTH]) * (SB_HEAD_DIM ** -0.5 * LOG2_E)).astype(BF16)
    ksb_ref[...] = _dot(h, w_in_ref[:, SB_WIDTH:2 * SB_WIDTH]).astype(BF16)
    vsb_ref[...] = _dot(h, w_in_ref[:, 2 * SB_WIDTH:3 * SB_WIDTH]).astype(BF16)

    ang = pos_ref[...] * invf_ref[...]
    cos_t = jnp.cos(ang)
    sin_t = jnp.sin(ang) * sgn_ref[...]
    lane = lax.broadcasted_iota(jnp.int32, (1, LANES), 1)
    first_half = lane < ROPE_LANE0 + ROPE_HALF

    def rope(v):
        partner = jnp.where(first_half, pltpu.roll(v, LANES - ROPE_HALF, 1), pltpu.roll(v, ROPE_HALF, 1))
        return v * cos_t + partner * sin_t

    cqn = _rms(c_q, g_cq_ref[...]).astype(BF16)
    q = _dot(cqn, w_uq_ref[...])
    ckvn = _rms(c_kv, g_ckv_ref[...]).astype(BF16)
    k_nope = _dot(ckvn, w_uk_ref[...])
    vm_ref[...] = _dot(ckvn, w_uv_ref[...]).astype(BF16)
    k_rope = rope(k_rope)
    for hd in range(MLA_HEADS):
        sl = slice(hd * MLA_HEAD_PAD, (hd + 1) * MLA_HEAD_PAD)
        qm_ref[:, sl] = (rope(q[:, sl]) * (MLA_QK_DIM ** -0.5 * LOG2_E)).astype(BF16)
        km_ref[:, sl] = (k_nope[:, sl] + k_rope).astype(BF16)


def _in_proj(x2d, pos, g_mix, w_in_p, g_cq, w_uq_p, g_ckv, w_uk_p, w_uv, invf, sgn, *, tm):
    t, d = x2d.shape
    row = lambda w: pl.BlockSpec((tm, w), lambda i: (i, 0))
    full = lambda a: pl.BlockSpec(a.shape, lambda i: (0, 0))
    outs = [(SB_WIDTH, BF16)] * 3 + [(MLA_HEADS * MLA_HEAD_PAD, BF16)] * 2 + [(MLA_WIDTH, BF16)]
    return pl.pallas_call(
        _in_proj_body,
        grid=(t // tm,),
        in_specs=[row(d), row(1), full(g_mix), full(w_in_p), full(g_cq), full(w_uq_p), full(g_ckv),
                  full(w_uk_p), full(w_uv), full(invf), full(sgn)],
        out_specs=[row(w) for w, _ in outs],
        out_shape=[jax.ShapeDtypeStruct((t, w), dt) for w, dt in outs],
        compiler_params=pltpu.CompilerParams(dimension_semantics=("arbitrary",),
                                             vmem_limit_bytes=VMEM_LIMIT_BYTES),
        name="in_proj",
    )(x2d, pos, g_mix, w_in_p, g_cq, w_uq_p, g_ckv, w_uk_p, w_uv, invf, sgn)


def _sb_body(q_ref, k_ref, v_ref, nsuf_ref, o_ref, qh_ref, tails_ref, acc_ref, top_ref, *, tq):
    i = pl.program_id(1)
    n_pairs = SB_HEADS // 2
    lane = lax.broadcasted_iota(jnp.int32, (1, LANES), 1)
    lo = lane < SB_HEAD_DIM
    nsuf = nsuf_ref[...]
    visible = (lax.broadcasted_iota(jnp.int32, (tq, tq), 1) < lax.broadcasted_iota(jnp.int32, (tq, tq), 0))

    for pair in range(n_pairs):
        q = q_ref[0, :, pair * LANES:(pair + 1) * LANES]
        zero_q = jnp.zeros_like(q)
        qh_ref[2 * pair] = jnp.where(lo, q, zero_q)
        qh_ref[2 * pair + 1] = jnp.where(lo, zero_q, q)

    def load_kv(j):
        k0 = pl.multiple_of(j * tq, tq)
        return [(k_ref[0, pl.ds(k0, tq), pair * LANES:(pair + 1) * LANES],
                 v_ref[0, pl.ds(k0, tq), pair * LANES:(pair + 1) * LANES]) for pair in range(n_pairs)]

    def sweep_tiles(kvs, state):
        first_diag = state is None
        tails, accs = ([None] * SB_HEADS, [None] * n_pairs) if first_diag else (list(state[0]), list(state[1]))

        def logits(it):
            kt, _ = it["kv"][it["h"] // 2]
            z = _dot_nt(qh_ref[it["h"]], kt)
            keep = jnp.maximum(z, 0.0) + jnp.log2(1.0 + jnp.exp2(-jnp.abs(z)))
            it["log_beta"] = z - keep
            it["keep"] = jnp.where(visible, keep, 0.0) if it["diag"] else keep

        def weights(it):
            h, keep = it["h"], it["keep"]
            tail = _dot(keep.astype(BF16), nsuf)
            total = jnp.broadcast_to(tail[:, 0:1] - keep[:, 0:1], (tq, LANES))
            if tails[h] is not None:
                tail = tail + jnp.tile(tails[h], (1, tq // LANES))
            a = jnp.exp2(it["log_beta"] + tail)
            it["a"] = (jnp.where(visible, a, 0.0) if it["diag"] else a).astype(BF16)
            tails[h] = total if tails[h] is None else tails[h] + total

        def values(it):
            pair, hd = divmod(it["h"], 2)
            _, vt = it["kv"][pair]
            zero_v = jnp.zeros_like(vt)
            v_h = jnp.where(lo, vt, zero_v) if hd == 0 else jnp.where(lo, zero_v, vt)
            pv = _dot(it["a"], v_h)
            accs[pair] = pv if accs[pair] is None else accs[pair] + pv

        items = [dict(h=h, kv=kv, diag=first_diag and n == 0) for n, kv in enumerate(kvs) for h in range(SB_HEADS)]
        _emit_skewed(items, (logits, weights, values))
        return tails, accs

    def save(state):
        tails, accs = state
        for h in range(SB_HEADS):
            tails_ref[h] = tails[h]
        for pair in range(n_pairs):
            acc_ref[pair] = accs[pair]
        top = functools.reduce(jnp.maximum, tails)
        top_ref[...] = jnp.max(top.reshape(tq // SUBLANES, SUBLANES, LANES), axis=0)

    @pl.when(i == 0)
    def _():
        save(sweep_tiles([load_kv(0)], None))

    @pl.when(i > 0)
    def _():
        save(sweep_tiles([load_kv(i), load_kv(i - 1)], None))

    def alive():
        return jnp.max(top_ref[...]) > SB_DEAD_LOG2

    def sweep(carry):
        j, _ = carry
        kv = load_kv(j)
        state = ([tails_ref[h] for h in range(SB_HEADS)], [acc_ref[pair] for pair in range(n_pairs)])
        save(sweep_tiles([kv], state))
        return j - 1, alive()

    @pl.when(i > 1)
    def _():
        lax.while_loop(lambda c: (c[0] >= 0) & c[1], sweep, (i - 2, alive()))

    for pair in range(n_pairs):
        o_ref[0, :, pair * LANES:(pair + 1) * LANES] = acc_ref[pair]


def _sb_attention(q, k, v, *, tq):
    b, s, w = q.shape
    idx = np.arange(tq)
    nsuf = jnp.asarray(np.where(idx[:, None] > idx[None, :], -1.0, 0.0), BF16)
    return pl.pallas_call(
        functools.partial(_sb_body, tq=tq),
        grid=(b, s // tq),
        in_specs=[pl.BlockSpec((1, tq, w), lambda bi, i: (bi, i, 0)),
                  pl.BlockSpec((1, s, w), lambda bi, i: (bi, 0, 0)),
                  pl.BlockSpec((1, s, w), lambda bi, i: (bi, 0, 0)),
                  pl.BlockSpec((tq, tq), lambda bi, i: (0, 0))],
        out_specs=pl.BlockSpec((1, tq, w), lambda bi, i: (bi, i, 0)),
        out_shape=jax.ShapeDtypeStruct((b, s, w), F32),
        scratch_shapes=[pltpu.VMEM((SB_HEADS, tq, LANES), BF16), pltpu.VMEM((SB_HEADS, tq, LANES), F32),
                        pltpu.VMEM((SB_HEADS // 2, tq, LANES), F32), pltpu.VMEM((SUBLANES, LANES), F32)],
        compiler_params=pltpu.CompilerParams(dimension_semantics=("arbitrary",) * 2,
                                             vmem_limit_bytes=VMEM_LIMIT_BYTES),
        name="sb_attention",
    )(q, k, v, nsuf)


def _mla_body(q_ref, k_ref, v_ref, o_ref, *, tq, n_q):
    i = pl.program_id(1)
    lane = lax.broadcasted_iota(jnp.int32, (1, LANES), 1)
    lo = lane < MLA_V_DIM
    visible = (lax.broadcasted_iota(jnp.int32, (tq, tq), 1) <= lax.broadcasted_iota(jnp.int32, (tq, tq), 0))

    def run(n_tiles):
        m = [None] * MLA_HEADS
        acc = [None] * MLA_HEADS

        def probs(it):
            hd, rows = it["hd"], it["rows"]
            q = q_ref[0, :, hd * MLA_HEAD_PAD:(hd + 1) * MLA_HEAD_PAD]
            kt = k_ref[0, rows, hd * MLA_HEAD_PAD:(hd + 1) * MLA_HEAD_PAD]
            s = _dot_nt(q, kt)
            if m[hd] is None:
                s = jnp.where(visible, s, -jnp.inf)
                m_new = jnp.broadcast_to(jnp.max(s, axis=-1, keepdims=True), (tq, LANES))
                it["alpha"] = None
            else:
                m_new = jnp.maximum(m[hd], jnp.max(s, axis=-1, keepdims=True))
                it["alpha"] = jnp.exp2(m[hd] - m_new)
            it["p"] = jnp.exp2(s - jnp.tile(m_new, (1, tq // LANES))).astype(BF16)
            m[hd] = m_new

        def values(it):
            hd = it["hd"]
            vt = v_ref[0, it["rows"], (hd // 2) * LANES:(hd // 2 + 1) * LANES]
            one = jnp.ones_like(vt)
            v_ext = jnp.where(lo, vt, one) if hd % 2 == 0 else jnp.where(lo, one, vt)
            pv = _dot(it["p"], v_ext)
            acc[hd] = pv if it["alpha"] is None else it["alpha"] * acc[hd] + pv

        order = [n_tiles - 1] + list(range(n_tiles - 1))
        items = [dict(hd=hd, rows=slice(j * tq, (j + 1) * tq)) for j in order for hd in range(MLA_HEADS)]
        _emit_skewed(items, (probs, values), skew=MLA_HEADS)
        for pair in range(MLA_HEADS // 2):
            a0, a1 = acc[2 * pair], acc[2 * pair + 1]
            o_ref[0, :, pair * LANES:(pair + 1) * LANES] = jnp.where(
                lo, a0 / pltpu.roll(a0, MLA_V_DIM, 1), a1 / pltpu.roll(a1, MLA_V_DIM, 1))

    for c in range(n_q):
        pl.when(i == c)(functools.partial(run, c + 1))


def _mla_attention(q, k, v, *, tq):
    b, s, wq = q.shape
    w = v.shape[-1]
    return pl.pallas_call(
        functools.partial(_mla_body, tq=tq, n_q=s // tq),
        grid=(b, s // tq),
        in_specs=[pl.BlockSpec((1, tq, wq), lambda bi, i: (bi, i, 0)),
                  pl.BlockSpec((1, s, wq), lambda bi, i: (bi, 0, 0)),
                  pl.BlockSpec((1, s, w), lambda bi, i: (bi, 0, 0))],
        out_specs=pl.BlockSpec((1, tq, w), lambda bi, i: (bi, i, 0)),
        out_shape=jax.ShapeDtypeStruct((b, s, w), F32),
        compiler_params=pltpu.CompilerParams(dimension_semantics=("arbitrary",) * 2,
                                             vmem_limit_bytes=VMEM_LIMIT_BYTES),
        name="mla_attention",
    )(q, k, v)


def _ffn_body(x_ref, osb_ref, omla_ref, g_sb_ref, g_mla_ref, w_out_ref, g_ffn_ref, w_up_ref, conv_ref,
              w_down_ref, g_final_ref, out_ref, h_ref, acc_ref, u_ref, *, tm, ck, tiles_per_seq):
    step = pl.program_id(0)
    halo = BF16_SUBLANES
    d = x_ref.shape[-1]

    @pl.when(step % tiles_per_seq == 0)
    def _():
        h_ref[0:halo, :] = jnp.zeros((halo, d), BF16)

    @pl.when(step % tiles_per_seq != 0)
    def _():
        h_ref[0:halo, :] = h_ref[tm:tm + halo, :]

    o = jnp.concatenate([_rms(osb_ref[...], g_sb_ref[...]), _rms(omla_ref[...], g_mla_ref[...])], axis=-1)
    x2 = x_ref[...] + _dot(o.astype(BF16), w_out_ref[...])
    acc_ref[...] = x2
    h_ref[halo:, :] = _rms(x2, g_ffn_ref[...]).astype(BF16)

    def cols(ref, c):
        return ref[:, c * ck:(c + 1) * ck], ref[:, D_FF + c * ck:D_FF + (c + 1) * ck]

    def up(c, slot):
        w_gate, w_val = cols(w_up_ref, c)
        h = h_ref[...]
        u_ref[slot, :, :ck] = _dot(h, w_gate)
        u_ref[slot, :, ck:] = _dot(h, w_val)

    def down(c, slot):
        u = u_ref[slot]
        cw = jnp.concatenate(cols(conv_ref, c), axis=1)
        y = u[halo:] * cw[CONV_WIDTH - 1:CONV_WIDTH] + cw[CONV_WIDTH:CONV_WIDTH + 1]
        for back in range(1, CONV_WIDTH):
            tap = cw[CONV_WIDTH - 1 - back:CONV_WIDTH - back]
            y = y + pltpu.roll(u, back, 0)[halo:] * tap
        gate, val = y[:, :ck], y[:, ck:]
        act = gate * (1.0 / (1.0 + jnp.exp(-gate))) * val
        acc_ref[...] += _dot(act.astype(BF16), w_down_ref[c * ck:(c + 1) * ck, :])

    n_chunks = D_FF // ck
    for c0 in range(0, n_chunks, FFN_GROUP):
        size = min(FFN_GROUP, n_chunks - c0)
        for k in range(size):
            up(c0 + k, k)
        for k in range(size):
            down(c0 + k, k)
    out_ref[...] = _rms(acc_ref[...], g_final_ref[...])


def _ffn(x2d, osb, omla, g_sb, g_mla, w_out, g_ffn, w_up, conv_rows, w_down, g_final, *, tm, ck, seq):
    t, d = x2d.shape
    ck2 = 2 * ck
    row = lambda w: pl.BlockSpec((tm, w), lambda i: (i, 0))
    const = lambda a: pl.BlockSpec(a.shape, lambda i: (0,) * a.ndim, pipeline_mode=pl.Buffered(1))
    return pl.pallas_call(
        functools.partial(_ffn_body, tm=tm, ck=ck, tiles_per_seq=seq // tm),
        grid=(t // tm,),
        in_specs=[row(d), row(SB_WIDTH), row(MLA_WIDTH), const(g_sb), const(g_mla), const(w_out),
                  const(g_ffn), const(w_up), const(conv_rows), const(w_down), const(g_final)],
        out_specs=row(d),
        out_shape=jax.ShapeDtypeStruct((t, d), F32),
        scratch_shapes=[pltpu.VMEM((tm + BF16_SUBLANES, d), BF16), pltpu.VMEM((tm, d), F32),
                        pltpu.VMEM((FFN_GROUP, tm + BF16_SUBLANES, ck2), F32)],
        compiler_params=pltpu.CompilerParams(dimension_semantics=("arbitrary",),
                                             vmem_limit_bytes=VMEM_LIMIT_BYTES),
        name="out_proj_ffn",
    )(x2d, osb, omla, g_sb, g_mla, w_out, g_ffn, w_up, conv_rows, w_down, g_final)


def _tiles(seq):
    return dict(tm_in=512, tq=256, tm_ffn=512, ck=256)


def kernel(x, positions, g_mix, w_in, g_cq, w_uq, g_ckv, w_ukv, g_sb_out, g_mla_out, w_out, g_ffn,
           w_up, conv_w, conv_b, w_down, g_final):
    b, s, d = x.shape
    depth = w_in.shape[0]
    t = b * s
    cfg = _tiles(s)
    assert s % cfg["tm_ffn"] == 0 and s % cfg["tq"] == 0 and t % cfg["tm_in"] == 0
    assert D_FF % cfg["ck"] == 0

    inv_freq = 1.0 / (ROPE_BASE ** (jnp.arange(ROPE_HALF, dtype=F32) * (2.0 / MLA_ROPE_DIM)))
    pad_l, pad_r = ROPE_LANE0, LANES - ROPE_LANE0 - MLA_ROPE_DIM
    invf = jnp.pad(jnp.tile(inv_freq, 2), (pad_l, pad_r))[None, :]
    sgn = jnp.pad(jnp.concatenate([-jnp.ones(ROPE_HALF, F32), jnp.ones(ROPE_HALF, F32)]), (pad_l, pad_r))[None, :]
    pos = positions.astype(F32).reshape(t, 1)

    x2d = x.reshape(t, d)
    for l in range(depth):
        n_main = 3 * SB_WIDTH + Q_LORA_RANK + KV_LORA_RANK
        w_in_p = jnp.concatenate(
            [w_in[l][:, :n_main], jnp.pad(w_in[l][:, n_main:], ((0, 0), (pad_l, pad_r)))], axis=1).astype(BF16)
        w_uq_p = jnp.pad(w_uq[l].reshape(Q_LORA_RANK, MLA_HEADS, MLA_QK_DIM),
                         ((0, 0), (0, 0), (0, MLA_HEAD_PAD - MLA_QK_DIM))
                         ).reshape(Q_LORA_RANK, MLA_HEADS * MLA_HEAD_PAD).astype(BF16)
        w_ukv_h = w_ukv[l].reshape(KV_LORA_RANK, MLA_HEADS, MLA_NOPE_DIM + MLA_V_DIM)
        w_uk_p = jnp.pad(w_ukv_h[:, :, :MLA_NOPE_DIM], ((0, 0), (0, 0), (0, MLA_HEAD_PAD - MLA_NOPE_DIM))
                         ).reshape(KV_LORA_RANK, MLA_HEADS * MLA_HEAD_PAD).astype(BF16)
        w_uv = w_ukv_h[:, :, MLA_NOPE_DIM:].reshape(KV_LORA_RANK, MLA_WIDTH).astype(BF16)

        qsb, ksb, vsb, qm, km, vm = _in_proj(
            x2d, pos, g_mix[l][None, :], w_in_p, g_cq[l][None, :], w_uq_p, g_ckv[l][None, :], w_uk_p, w_uv,
            invf, sgn, tm=cfg["tm_in"])

        o_sb = _sb_attention(qsb.reshape(b, s, -1), ksb.reshape(b, s, -1), vsb.reshape(b, s, -1), tq=cfg["tq"])
        o_mla = _mla_attention(qm.reshape(b, s, -1), km.reshape(b, s, -1), vm.reshape(b, s, -1), tq=cfg["tq"])

        conv_rows = jnp.concatenate([conv_w[l], conv_b[l][None, :]], axis=0)
        conv_rows = jnp.pad(conv_rows, ((0, 8 - conv_rows.shape[0]), (0, 0)))
        last = l == depth - 1
        g_last = g_final[None, :] if last else None
        assert last, "the fused final rmsnorm assumes a single layer"
        x2d = _ffn(x2d, o_sb.reshape(t, -1), o_mla.reshape(t, -1), g_sb_out[l][None, :], g_mla_out[l][None, :],
                   w_out[l].astype(BF16), g_ffn[l][None, :], w_up[l].astype(BF16), conv_rows,
                   w_down[l].astype(BF16), g_last, tm=cfg["tm_ffn"], ck=cfg["ck"], seq=s)
    return x2d.reshape(b, s, d)
```

```python
import functools

import numpy as np
import jax
import jax.numpy as jnp
from jax import lax
from jax.experimental import pallas as pl
from jax.experimental.pallas import tpu as pltpu

SB_HEADS = 8
SB_HEAD_DIM = 64
MLA_HEADS = 8
MLA_NOPE_DIM = 64
MLA_ROPE_DIM = 32
MLA_V_DIM = 64
Q_LORA_RANK = 384
KV_LORA_RANK = 256
D_FF = 2816
CONV_WIDTH = 3
ROPE_BASE = 10000.0
EPS = 1e-6

SB_WIDTH = SB_HEADS * SB_HEAD_DIM
MLA_WIDTH = MLA_HEADS * MLA_V_DIM
MLA_QK_DIM = MLA_NOPE_DIM + MLA_ROPE_DIM

LANES = 128
SUBLANES = 8
BF16_SUBLANES = 16
VMEM_LIMIT_BYTES = 56 * 1024 * 1024

MLA_HEAD_PAD = LANES
ROPE_LANE0 = MLA_NOPE_DIM
ROPE_HALF = MLA_ROPE_DIM // 2

BF16 = jnp.bfloat16
F32 = jnp.float32

LOG2_E = 1.4426950408889634
SB_DEAD_LOG2 = -152.0
FFN_GROUP = 6


def _rms(x, g):
    return x * lax.rsqrt(jnp.mean(x * x, axis=-1, keepdims=True) + EPS) * g


def _dot(a, b):
    return jnp.dot(a, b, preferred_element_type=F32)


def _dot_nt(a, b):
    return lax.dot_general(a, b, (((1,), (1,)), ((), ())), preferred_element_type=F32)


def _emit_skewed(items, stages, skew=1):
    for step in range(len(items) + (len(stages) - 1) * skew):
        for k, stage in enumerate(stages):
            n = step - k * skew
            if 0 <= n < len(items):
                stage(items[n])


def _in_proj_body(x_ref, pos_ref, g_mix_ref, w_in_ref, g_cq_ref, w_uq_ref, g_ckv_ref, w_uk_ref,
                  w_uv_ref, invf_ref, sgn_ref,
                  qsb_ref, ksb_ref, vsb_ref, qm_ref, km_ref, vm_ref):
    h = _rms(x_ref[...], g_mix_ref[...]).astype(BF16)

    c0 = 3 * SB_WIDTH
    c_q = _dot(h, w_in_ref[:, c0:c0 + Q_LORA_RANK])
    c0 += Q_LORA_RANK
    c_kv = _dot(h, w_in_ref[:, c0:c0 + KV_LORA_RANK])
    c0 += KV_LORA_RANK
    k_rope = _dot(h, w_in_ref[:, c0:c0 + LANES])
    qsb_ref[...] = (_dot(h, w_in_ref[:, 0:SB_WIDTH]) * (SB_HEAD_DIM ** -0.5 * LOG2_E)).astype(BF16)
    ksb_ref[...] = _dot(h, w_in_ref[:, SB_WIDTH:2 * SB_WIDTH]).astype(BF16)
    vsb_ref[...] = _dot(h, w_in_ref[:, 2 * SB_WIDTH:3 * SB_WIDTH]).astype(BF16)

    ang = pos_ref[...] * invf_ref[...]
    cos_t = jnp.cos(ang)
    sin_t = jnp.sin(ang) * sgn_ref[...]
    lane = lax.broadcasted_iota(jnp.int32, (1, LANES), 1)
    first_half = lane < ROPE_LANE0 + ROPE_HALF

    def rope(v):
        partner = jnp.where(first_half, pltpu.roll(v, LANES - ROPE_HALF, 1), pltpu.roll(v, ROPE_HALF, 1))
        return v * cos_t + partner * sin_t

    cqn = _rms(c_q, g_cq_ref[...]).astype(BF16)
    q = _dot(cqn, w_uq_ref[...])
    ckvn = _rms(c_kv, g_ckv_ref[...]).astype(BF16)
    k_nope = _dot(ckvn, w_uk_ref[...])
    vm_ref[...] = _dot(ckvn, w_uv_ref[...]).astype(BF16)
    k_rope = rope(k_rope)
    for hd in range(MLA_HEADS):
        sl = slice(hd * MLA_HEAD_PAD, (hd + 1) * MLA_HEAD_PAD)
        qm_ref[:, sl] = (rope(q[:, sl]) * (MLA_QK_DIM ** -0.5 * LOG2_E)).astype(BF16)
        km_ref[:, sl] = (k_nope[:, sl] + k_rope).astype(BF16)


def _in_proj(x2d, pos, g_mix, w_in_p, g_cq, w_uq_p, g_ckv, w_uk_p, w_uv, invf, sgn, *, tm):
    t, d = x2d.shape
    row = lambda w: pl.BlockSpec((tm, w), lambda i: (i, 0))
    full = lambda a: pl.BlockSpec(a.shape, lambda i: (0, 0))
    outs = [(SB_WIDTH, BF16)] * 3 + [(MLA_HEADS * MLA_HEAD_PAD, BF16)] * 2 + [(MLA_WIDTH, BF16)]
    return pl.pallas_call(
        _in_proj_body,
        grid=(t // tm,),
        in_specs=[row(d), row(1), full(g_mix), full(w_in_p), full(g_cq), full(w_uq_p), full(g_ckv),
                  full(w_uk_p), full(w_uv), full(invf), full(sgn)],
        out_specs=[row(w) for w, _ in outs],
        out_shape=[jax.ShapeDtypeStruct((t, w), dt) for w, dt in outs],
        compiler_params=pltpu.CompilerParams(dimension_semantics=("arbitrary",),
                                             vmem_limit_bytes=VMEM_LIMIT_BYTES),
        name="in_proj",
    )(x2d, pos, g_mix, w_in_p, g_cq, w_uq_p, g_ckv, w_uk_p, w_uv, invf, sgn)


def _sb_body(q_ref, k_ref, v_ref, nsuf_ref, o_ref, qh_ref, tails_ref, acc_ref, top_ref, *, tq):
    i = pl.program_id(1)
    n_pairs = SB_HEADS // 2
    lane = lax.broadcasted_iota(jnp.int32, (1, LANES), 1)
    lo = lane < SB_HEAD_DIM
    nsuf = nsuf_ref[...]
    visible = (lax.broadcasted_iota(jnp.int32, (tq, tq), 1) < lax.broadcasted_iota(jnp.int32, (tq, tq), 0))

    for pair in range(n_pairs):
        q = q_ref[0, :, pair * LANES:(pair + 1) * LANES]
        zero_q = jnp.zeros_like(q)
        qh_ref[2 * pair] = jnp.where(lo, q, zero_q)
        qh_ref[2 * pair + 1] = jnp.where(lo, zero_q, q)

    def load_kv(j):
        k0 = pl.multiple_of(j * tq, tq)
        return [(k_ref[0, pl.ds(k0, tq), pair * LANES:(pair + 1) * LANES],
                 v_ref[0, pl.ds(k0, tq), pair * LANES:(pair + 1) * LANES]) for pair in range(n_pairs)]

    def sweep_tiles(kvs, state):
        first_diag = state is None
        tails, accs = ([None] * SB_HEADS, [None] * n_pairs) if first_diag else (list(state[0]), list(state[1]))

        def logits(it):
            kt, _ = it["kv"][it["h"] // 2]
            z = _dot_nt(qh_ref[it["h"]], kt)
            keep = jnp.maximum(z, 0.0) + jnp.log2(1.0 + jnp.exp2(-jnp.abs(z)))
            it["log_beta"] = z - keep
            it["keep"] = jnp.where(visible, keep, 0.0) if it["diag"] else keep

        def weights(it):
            h, keep = it["h"], it["keep"]
            tail = _dot(keep.astype(BF16), nsuf)
            total = jnp.broadcast_to(tail[:, 0:1] - keep[:, 0:1], (tq, LANES))
            if tails[h] is not None:
                tail = tail + jnp.tile(tails[h], (1, tq // LANES))
            a = jnp.exp2(it["log_beta"] + tail)
            it["a"] = (jnp.where(visible, a, 0.0) if it["diag"] else a).astype(BF16)
            tails[h] = total if tails[h] is None else tails[h] + total

        def values(it):
            pair, hd = divmod(it["h"], 2)
            _, vt = it["kv"][pair]
            zero_v = jnp.zeros_like(vt)
            v_h = jnp.where(lo, vt, zero_v) if hd == 0 else jnp.where(lo, zero_v, vt)
            pv = _dot(it["a"], v_h)
            accs[pair] = pv if accs[pair] is None else accs[pair] + pv

        items = [dict(h=h, kv=kv, diag=first_diag and n == 0) for n, kv in enumerate(kvs) for h in range(SB_HEADS)]
        _emit_skewed(items, (logits, weights, values))
        return tails, accs

    def save(state):
        tails, accs = state
        for h in range(SB_HEADS):
            tails_ref[h] = tails[h]
        for pair in range(n_pairs):
            acc_ref[pair] = accs[pair]
        top = functools.reduce(jnp.maximum, tails)
        top_ref[...] = jnp.max(top.reshape(tq // SUBLANES, SUBLANES, LANES), axis=0)

    @pl.when(i == 0)
    def _():
        save(sweep_tiles([load_kv(0)], None))

    @pl.when(i > 0)
    def _():
        save(sweep_tiles([load_kv(i), load_kv(i - 1)], None))

    def alive():
        return jnp.max(top_ref[...]) > SB_DEAD_LOG2

    def sweep(carry):
        j, _ = carry
        kv = load_kv(j)
        state = ([tails_ref[h] for h in range(SB_HEADS)], [acc_ref[pair] for pair in range(n_pairs)])
        save(sweep_tiles([kv], state))
        return j - 1, alive()

    @pl.when(i > 1)
    def _():
        lax.while_loop(lambda c: (c[0] >= 0) & c[1], sweep, (i - 2, alive()))

    for pair in range(n_pairs):
        o_ref[0, :, pair * LANES:(pair + 1) * LANES] = acc_ref[pair]


def _sb_attention(q, k, v, *, tq):
    b, s, w = q.shape
    idx = np.arange(tq)
    nsuf = jnp.asarray(np.where(idx[:, None] > idx[None, :], -1.0, 0.0), BF16)
    return pl.pallas_call(
        functools.partial(_sb_body, tq=tq),
        grid=(b, s // tq),
        in_specs=[pl.BlockSpec((1, tq, w), lambda bi, i: (bi, i, 0)),
                  pl.BlockSpec((1, s, w), lambda bi, i: (bi, 0, 0)),
                  pl.BlockSpec((1, s, w), lambda bi, i: (bi, 0, 0)),
                  pl.BlockSpec((tq, tq), lambda bi, i: (0, 0))],
        out_specs=pl.BlockSpec((1, tq, w), lambda bi, i: (bi, i, 0)),
        out_shape=jax.ShapeDtypeStruct((b, s, w), F32),
        scratch_shapes=[pltpu.VMEM((SB_HEADS, tq, LANES), BF16), pltpu.VMEM((SB_HEADS, tq, LANES), F32),
                        pltpu.VMEM((SB_HEADS // 2, tq, LANES), F32), pltpu.VMEM((SUBLANES, LANES), F32)],
        compiler_params=pltpu.CompilerParams(dimension_semantics=("arbitrary",) * 2,
                                             vmem_limit_bytes=VMEM_LIMIT_BYTES),
        name="sb_attention",
    )(q, k, v, nsuf)


def _mla_body(q_ref, k_ref, v_ref, o_ref, *, tq, n_q):
    i = pl.program_id(1)
    lane = lax.broadcasted_iota(jnp.int32, (1, LANES), 1)
    lo = lane < MLA_V_DIM
    visible = (lax.broadcasted_iota(jnp.int32, (tq, tq), 1) <= lax.broadcasted_iota(jnp.int32, (tq, tq), 0))

    def run(n_tiles):
        m = [None] * MLA_HEADS
        acc = [None] * MLA_HEADS

        def probs(it):
            hd, rows = it["hd"], it["rows"]
            q = q_ref[0, :, hd * MLA_HEAD_PAD:(hd + 1) * MLA_HEAD_PAD]
            kt = k_ref[0, rows, hd * MLA_HEAD_PAD:(hd + 1) * MLA_HEAD_PAD]
            s = _dot_nt(q, kt)
            if m[hd] is None:
                s = jnp.where(visible, s, -jnp.inf)
                m_new = jnp.broadcast_to(jnp.max(s, axis=-1, keepdims=True), (tq, LANES))
                it["alpha"] = None
            else:
                m_new = jnp.maximum(m[hd], jnp.max(s, axis=-1, keepdims=True))
                it["alpha"] = jnp.exp2(m[hd] - m_new)
            it["p"] = jnp.exp2(s - jnp.tile(m_new, (1, tq // LANES))).astype(BF16)
            m[hd] = m_new

        def values(it):
            hd = it["hd"]
            vt = v_ref[0, it["rows"], (hd // 2) * LANES:(hd // 2 + 1) * LANES]
            one = jnp.ones_like(vt)
            v_ext = jnp.where(lo, vt, one) if hd % 2 == 0 else jnp.where(lo, one, vt)
            pv = _dot(it["p"], v_ext)
            acc[hd] = pv if it["alpha"] is None else it["alpha"] * acc[hd] + pv

        order = [n_tiles - 1] + list(range(n_tiles - 1))
        items = [dict(hd=hd, rows=slice(j * tq, (j + 1) * tq)) for j in order for hd in range(MLA_HEADS)]
        _emit_skewed(items, (probs, values), skew=MLA_HEADS)
        for pair in range(MLA_HEADS // 2):
            a0, a1 = acc[2 * pair], acc[2 * pair + 1]
            o_ref[0, :, pair * LANES:(pair + 1) * LANES] = jnp.where(
                lo, a0 / pltpu.roll(a0, MLA_V_DIM, 1), a1 / pltpu.roll(a1, MLA_V_DIM, 1))

    for c in range(n_q):
        pl.when(i == c)(functools.partial(run, c + 1))


def _mla_attention(q, k, v, *, tq):
    b, s, wq = q.shape
    w = v.shape[-1]
    return pl.pallas_call(
        functools.partial(_mla_body, tq=tq, n_q=s // tq),
        grid=(b, s // tq),
        in_specs=[pl.BlockSpec((1, tq, wq), lambda bi, i: (bi, i, 0)),
                  pl.BlockSpec((1, s, wq), lambda bi, i: (bi, 0, 0)),
                  pl.BlockSpec((1, s, w), lambda bi, i: (bi, 0, 0))],
        out_specs=pl.BlockSpec((1, tq, w), lambda bi, i: (bi, i, 0)),
        out_shape=jax.ShapeDtypeStruct((b, s, w), F32),
        compiler_params=pltpu.CompilerParams(dimension_semantics=("arbitrary",) * 2,
                                             vmem_limit_bytes=VMEM_LIMIT_BYTES),
        name="mla_attention",
    )(q, k, v)


def _ffn_body(x_ref, osb_ref, omla_ref, g_sb_ref, g_mla_ref, w_out_ref, g_ffn_ref, w_up_ref, conv_ref,
              w_down_ref, g_final_ref, out_ref, h_ref, acc_ref, u_ref, *, tm, ck, tiles_per_seq):
    step = pl.program_id(0)
    halo = BF16_SUBLANES
    d = x_ref.shape[-1]

    @pl.when(step % tiles_per_seq == 0)
    def _():
        h_ref[0:halo, :] = jnp.zeros((halo, d), BF16)

    @pl.when(step % tiles_per_seq != 0)
    def _():
        h_ref[0:halo, :] = h_ref[tm:tm + halo, :]

    o = jnp.concatenate([_rms(osb_ref[...], g_sb_ref[...]), _rms(omla_ref[...], g_mla_ref[...])], axis=-1)
    x2 = x_ref[...] + _dot(o.astype(BF16), w_out_ref[...])
    acc_ref[...] = x2
    h_ref[halo:, :] = _rms(x2, g_ffn_ref[...]).astype(BF16)

    def cols(ref, c):
        return ref[:, c * ck:(c + 1) * ck], ref[:, D_FF + c * ck:D_FF + (c + 1) * ck]

    def up(c, slot):
        w_gate, w_val = cols(w_up_ref, c)
        h = h_ref[...]
        u_ref[slot, :, :ck] = _dot(h, w_gate)
        u_ref[slot, :, ck:] = _dot(h, w_val)

    def down(c, slot):
        u = u_ref[slot]
        cw = jnp.concatenate(cols(conv_ref, c), axis=1)
        y = u[halo:] * cw[CONV_WIDTH - 1:CONV_WIDTH] + cw[CONV_WIDTH:CONV_WIDTH + 1]
        for back in range(1, CONV_WIDTH):
            tap = cw[CONV_WIDTH - 1 - back:CONV_WIDTH - back]
            y = y + pltpu.roll(u, back, 0)[halo:] * tap
        gate, val = y[:, :ck], y[:, ck:]
        act = gate * (1.0 / (1.0 + jnp.exp(-gate))) * val
        acc_ref[...] += _dot(act.astype(BF16), w_down_ref[c * ck:(c + 1) * ck, :])

    n_chunks = D_FF // ck
    for c0 in range(0, n_chunks, FFN_GROUP):
        size = min(FFN_GROUP, n_chunks - c0)
        for k in range(size):
            up(c0 + k, k)
        for k in range(size):
            down(c0 + k, k)
    out_ref[...] = _rms(acc_ref[...], g_final_ref[...])


def _ffn(x2d, osb, omla, g_sb, g_mla, w_out, g_ffn, w_up, conv_rows, w_down, g_final, *, tm, ck, seq):
    t, d = x2d.shape
    ck2 = 2 * ck
    row = lambda w: pl.BlockSpec((tm, w), lambda i: (i, 0))
    const = lambda a: pl.BlockSpec(a.shape, lambda i: (0,) * a.ndim, pipeline_mode=pl.Buffered(1))
    return pl.pallas_call(
        functools.partial(_ffn_body, tm=tm, ck=ck, tiles_per_seq=seq // tm),
        grid=(t // tm,),
        in_specs=[row(d), row(SB_WIDTH), row(MLA_WIDTH), const(g_sb), const(g_mla), const(w_out),
                  const(g_ffn), const(w_up), const(conv_rows), const(w_down), const(g_final)],
        out_specs=row(d),
        out_shape=jax.ShapeDtypeStruct((t, d), F32),
        scratch_shapes=[pltpu.VMEM((tm + BF16_SUBLANES, d), BF16), pltpu.VMEM((tm, d), F32),
                        pltpu.VMEM((FFN_GROUP, tm + BF16_SUBLANES, ck2), F32)],
        compiler_params=pltpu.CompilerParams(dimension_semantics=("arbitrary",),
                                             vmem_limit_bytes=VMEM_LIMIT_BYTES),
        name="out_proj_ffn",
    )(x2d, osb, omla, g_sb, g_mla, w_out, g_ffn, w_up, conv_rows, w_down, g_final)


def _tiles(seq):
    return dict(tm_in=1024, tq=256, tm_ffn=512, ck=256)


def kernel(x, positions, g_mix, w_in, g_cq, w_uq, g_ckv, w_ukv, g_sb_out, g_mla_out, w_out, g_ffn,
           w_up, conv_w, conv_b, w_down, g_final):
    b, s, d = x.shape
    depth = w_in.shape[0]
    t = b * s
    cfg = _tiles(s)
    assert s % cfg["tm_ffn"] == 0 and s % cfg["tq"] == 0 and t % cfg["tm_in"] == 0
    assert D_FF % cfg["ck"] == 0

    inv_freq = 1.0 / (ROPE_BASE ** (jnp.arange(ROPE_HALF, dtype=F32) * (2.0 / MLA_ROPE_DIM)))
    pad_l, pad_r = ROPE_LANE0, LANES - ROPE_LANE0 - MLA_ROPE_DIM
    invf = jnp.pad(jnp.tile(inv_freq, 2), (pad_l, pad_r))[None, :]
    sgn = jnp.pad(jnp.concatenate([-jnp.ones(ROPE_HALF, F32), jnp.ones(ROPE_HALF, F32)]), (pad_l, pad_r))[None, :]
    pos = positions.astype(F32).reshape(t, 1)

    x2d = x.reshape(t, d)
    for l in range(depth):
        n_main = 3 * SB_WIDTH + Q_LORA_RANK + KV_LORA_RANK
        w_in_p = jnp.concatenate(
            [w_in[l][:, :n_main], jnp.pad(w_in[l][:, n_main:], ((0, 0), (pad_l, pad_r)))], axis=1).astype(BF16)
        w_uq_p = jnp.pad(w_uq[l].reshape(Q_LORA_RANK, MLA_HEADS, MLA_QK_DIM),
                         ((0, 0), (0, 0), (0, MLA_HEAD_PAD - MLA_QK_DIM))
                         ).reshape(Q_LORA_RANK, MLA_HEADS * MLA_HEAD_PAD).astype(BF16)
        w_ukv_h = w_ukv[l].reshape(KV_LORA_RANK, MLA_HEADS, MLA_NOPE_DIM + MLA_V_DIM)
        w_uk_p = jnp.pad(w_ukv_h[:, :, :MLA_NOPE_DIM], ((0, 0), (0, 0), (0, MLA_HEAD_PAD - MLA_NOPE_DIM))
                         ).reshape(KV_LORA_RANK, MLA_HEADS * MLA_HEAD_PAD).astype(BF16)
        w_uv = w_ukv_h[:, :, MLA_NOPE_DIM:].reshape(KV_LORA_RANK, MLA_WIDTH).astype(BF16)

        qsb, ksb, vsb, qm, km, vm = _in_proj(
            x2d, pos, g_mix[l][None, :], w_in_p, g_cq[l][None, :], w_uq_p, g_ckv[l][None, :], w_uk_p, w_uv,
            invf, sgn, tm=cfg["tm_in"])

        o_sb = _sb_attention(qsb.reshape(b, s, -1), ksb.reshape(b, s, -1), vsb.reshape(b, s, -1), tq=cfg["tq"])
        o_mla = _mla_attention(qm.reshape(b, s, -1), km.reshape(b, s, -1), vm.reshape(b, s, -1), tq=cfg["tq"])

        conv_rows = jnp.concatenate([conv_w[l], conv_b[l][None, :]], axis=0)
        conv_rows = jnp.pad(conv_rows, ((0, 8 - conv_rows.shape[0]), (0, 0)))
        last = l == depth - 1
        g_last = g_final[None, :] if last else None
        assert last, "the fused final rmsnorm assumes a single layer"
        x2d = _ffn(x2d, o_sb.reshape(t, -1), o_mla.reshape(t, -1), g_sb_out[l][None, :], g_mla_out[l][None, :],
                   w_out[l].astype(BF16), g_ffn[l][None, :], w_up[l].astype(BF16), conv_rows,
                   w_down[l].astype(BF16), g_last, tm=cfg["tm_ffn"], ck=cfg["ck"], seq=s)
    return x2d.reshape(b, s, d)
```

```python
import functools

import numpy as np
import jax
import jax.numpy as jnp
from jax import lax
from jax.experimental import pallas as pl
from jax.experimental.pallas import tpu as pltpu

SB_HEADS = 8
SB_HEAD_DIM = 64
MLA_HEADS = 8
MLA_NOPE_DIM = 64
MLA_ROPE_DIM = 32
MLA_V_DIM = 64
Q_LORA_RANK = 384
KV_LORA_RANK = 256
D_FF = 2816
CONV_WIDTH = 3
ROPE_BASE = 10000.0
EPS = 1e-6

SB_WIDTH = SB_HEADS * SB_HEAD_DIM
MLA_WIDTH = MLA_HEADS * MLA_V_DIM
MLA_QK_DIM = MLA_NOPE_DIM + MLA_ROPE_DIM

LANES = 128
SUBLANES = 8
BF16_SUBLANES = 16
VMEM_LIMIT_BYTES = 56 * 1024 * 1024

MLA_HEAD_PAD = LANES
ROPE_LANE0 = MLA_NOPE_DIM
ROPE_HALF = MLA_ROPE_DIM // 2

BF16 = jnp.bfloat16
F32 = jnp.float32

LOG2_E = 1.4426950408889634
SB_DEAD_LOG2 = -152.0
FFN_GROUP = 6


def _rms(x, g):
    return x * lax.rsqrt(jnp.mean(x * x, axis=-1, keepdims=True) + EPS) * g


def _dot(a, b):
    return jnp.dot(a, b, preferred_element_type=F32)


def _dot_nt(a, b):
    return lax.dot_general(a, b, (((1,), (1,)), ((), ())), preferred_element_type=F32)


def _emit_skewed(items, stages, skew=1):
    for step in range(len(items) + (len(stages) - 1) * skew):
        for k, stage in enumerate(stages):
            n = step - k * skew
            if 0 <= n < len(items):
                stage(items[n])


def _in_proj_body(x_ref, pos_ref, g_mix_ref, w_in_ref, g_cq_ref, w_uq_ref, g_ckv_ref, w_uk_ref,
                  w_uv_ref, invf_ref, sgn_ref,
                  qsb_ref, ksb_ref, vsb_ref, qm_ref, km_ref, vm_ref):
    h = _rms(x_ref[...], g_mix_ref[...]).astype(BF16)

    c0 = 3 * SB_WIDTH
    c_q = _dot(h, w_in_ref[:, c0:c0 + Q_LORA_RANK])
    c0 += Q_LORA_RANK
    c_kv = _dot(h, w_in_ref[:, c0:c0 + KV_LORA_RANK])
    c0 += KV_LORA_RANK
    k_rope = _dot(h, w_in_ref[:, c0:c0 + LANES])
    qsb_ref[...] = (_dot(h, w_in_ref[:, 0:SB_WIDTH]) * (SB_HEAD_DIM ** -0.5 * LOG2_E)).astype(BF16)
    ksb_ref[...] = _dot(h, w_in_ref[:, SB_WIDTH:2 * SB_WIDTH]).astype(BF16)
    vsb_ref[...] = _dot(h, w_in_ref[:, 2 * SB_WIDTH:3 * SB_WIDTH]).astype(BF16)

    ang = pos_ref[...] * invf_ref[...]
    cos_t = jnp.cos(ang)
    sin_t = jnp.sin(ang) * sgn_ref[...]
    lane = lax.broadcasted_iota(jnp.int32, (1, LANES), 1)
    first_half = lane < ROPE_LANE0 + ROPE_HALF

    def rope(v):
        partner = jnp.where(first_half, pltpu.roll(v, LANES - ROPE_HALF, 1), pltpu.roll(v, ROPE_HALF, 1))
        return v * cos_t + partner * sin_t

    cqn = _rms(c_q, g_cq_ref[...]).astype(BF16)
    q = _dot(cqn, w_uq_ref[...])
    ckvn = _rms(c_kv, g_ckv_ref[...]).astype(BF16)
    k_nope = _dot(ckvn, w_uk_ref[...])
    vm_ref[...] = _dot(ckvn, w_uv_ref[...]).astype(BF16)
    k_rope = rope(k_rope)
    for hd in range(MLA_HEADS):
        sl = slice(hd * MLA_HEAD_PAD, (hd + 1) * MLA_HEAD_PAD)
        qm_ref[:, sl] = (rope(q[:, sl]) * (MLA_QK_DIM ** -0.5 * LOG2_E)).astype(BF16)
        km_ref[:, sl] = (k_nope[:, sl] + k_rope).astype(BF16)


def _in_proj(x2d, pos, g_mix, w_in_p, g_cq, w_uq_p, g_ckv, w_uk_p, w_uv, invf, sgn, *, tm):
    t, d = x2d.shape
    row = lambda w: pl.BlockSpec((tm, w), lambda i: (i, 0))
    full = lambda a: pl.BlockSpec(a.shape, lambda i: (0, 0))
    outs = [(SB_WIDTH, BF16)] * 3 + [(MLA_HEADS * MLA_HEAD_PAD, BF16)] * 2 + [(MLA_WIDTH, BF16)]
    return pl.pallas_call(
        _in_proj_body,
        grid=(t // tm,),
        in_specs=[row(d), row(1), full(g_mix), full(w_in_p), full(g_cq), full(w_uq_p), full(g_ckv),
                  full(w_uk_p), full(w_uv), full(invf), full(sgn)],
        out_specs=[row(w) for w, _ in outs],
        out_shape=[jax.ShapeDtypeStruct((t, w), dt) for w, dt in outs],
        compiler_params=pltpu.CompilerParams(dimension_semantics=("arbitrary",),
                                             vmem_limit_bytes=VMEM_LIMIT_BYTES),
        name="in_proj",
    )(x2d, pos, g_mix, w_in_p, g_cq, w_uq_p, g_ckv, w_uk_p, w_uv, invf, sgn)


def _sb_body(q_ref, k_ref, v_ref, nsuf_ref, o_ref, qh_ref, tails_ref, acc_ref, top_ref, *, tq):
    i = pl.program_id(1)
    n_pairs = SB_HEADS // 2
    lane = lax.broadcasted_iota(jnp.int32, (1, LANES), 1)
    lo = lane < SB_HEAD_DIM
    nsuf = nsuf_ref[...]
    visible = (lax.broadcasted_iota(jnp.int32, (tq, tq), 1) < lax.broadcasted_iota(jnp.int32, (tq, tq), 0))

    for pair in range(n_pairs):
        q = q_ref[0, :, pair * LANES:(pair + 1) * LANES]
        zero_q = jnp.zeros_like(q)
        qh_ref[2 * pair] = jnp.where(lo, q, zero_q)
        qh_ref[2 * pair + 1] = jnp.where(lo, zero_q, q)

    def load_kv(j):
        k0 = pl.multiple_of(j * tq, tq)
        return [(k_ref[0, pl.ds(k0, tq), pair * LANES:(pair + 1) * LANES],
                 v_ref[0, pl.ds(k0, tq), pair * LANES:(pair + 1) * LANES]) for pair in range(n_pairs)]

    def sweep_tiles(kvs, state):
        first_diag = state is None
        tails, accs = ([None] * SB_HEADS, [None] * n_pairs) if first_diag else (list(state[0]), list(state[1]))

        def logits(it):
            kt, _ = it["kv"][it["h"] // 2]
            z = _dot_nt(qh_ref[it["h"]], kt)
            keep = jnp.maximum(z, 0.0) + jnp.log2(1.0 + jnp.exp2(-jnp.abs(z)))
            it["log_beta"] = z - keep
            it["keep"] = jnp.where(visible, keep, 0.0) if it["diag"] else keep

        def weights(it):
            h, keep = it["h"], it["keep"]
            tail = _dot(keep.astype(BF16), nsuf)
            total = jnp.broadcast_to(tail[:, 0:1] - keep[:, 0:1], (tq, LANES))
            if tails[h] is not None:
                tail = tail + jnp.tile(tails[h], (1, tq // LANES))
            a = jnp.exp2(it["log_beta"] + tail)
            it["a"] = (jnp.where(visible, a, 0.0) if it["diag"] else a).astype(BF16)
            tails[h] = total if tails[h] is None else tails[h] + total

        def values(it):
            pair, hd = divmod(it["h"], 2)
            _, vt = it["kv"][pair]
            zero_v = jnp.zeros_like(vt)
            v_h = jnp.where(lo, vt, zero_v) if hd == 0 else jnp.where(lo, zero_v, vt)
            pv = _dot(it["a"], v_h)
            accs[pair] = pv if accs[pair] is None else accs[pair] + pv

        items = [dict(h=h, kv=kv, diag=first_diag and n == 0) for n, kv in enumerate(kvs) for h in range(SB_HEADS)]
        _emit_skewed(items, (logits, weights, values))
        return tails, accs

    def save(state):
        tails, accs = state
        for h in range(SB_HEADS):
            tails_ref[h] = tails[h]
        for pair in range(n_pairs):
            acc_ref[pair] = accs[pair]
        top = functools.reduce(jnp.maximum, tails)
        top_ref[...] = jnp.max(top.reshape(tq // SUBLANES, SUBLANES, LANES), axis=0)

    @pl.when(i == 0)
    def _():
        save(sweep_tiles([load_kv(0)], None))

    @pl.when(i > 0)
    def _():
        save(sweep_tiles([load_kv(i), load_kv(i - 1)], None))

    def alive():
        return jnp.max(top_ref[...]) > SB_DEAD_LOG2

    def sweep(carry):
        j, _ = carry
        kv = load_kv(j)
        state = ([tails_ref[h] for h in range(SB_HEADS)], [acc_ref[pair] for pair in range(n_pairs)])
        save(sweep_tiles([kv], state))
        return j - 1, alive()

    @pl.when(i > 1)
    def _():
        lax.while_loop(lambda c: (c[0] >= 0) & c[1], sweep, (i - 2, alive()))

    for pair in range(n_pairs):
        o_ref[0, :, pair * LANES:(pair + 1) * LANES] = acc_ref[pair].astype(o_ref.dtype)


def _sb_attention(q, k, v, *, tq):
    b, s, w = q.shape
    idx = np.arange(tq)
    nsuf = jnp.asarray(np.where(idx[:, None] > idx[None, :], -1.0, 0.0), BF16)
    return pl.pallas_call(
        functools.partial(_sb_body, tq=tq),
        grid=(b, s // tq),
        in_specs=[pl.BlockSpec((1, tq, w), lambda bi, i: (bi, i, 0)),
                  pl.BlockSpec((1, s, w), lambda bi, i: (bi, 0, 0)),
                  pl.BlockSpec((1, s, w), lambda bi, i: (bi, 0, 0)),
                  pl.BlockSpec((tq, tq), lambda bi, i: (0, 0))],
        out_specs=pl.BlockSpec((1, tq, w), lambda bi, i: (bi, i, 0)),
        out_shape=jax.ShapeDtypeStruct((b, s, w), BF16),
        scratch_shapes=[pltpu.VMEM((SB_HEADS, tq, LANES), BF16), pltpu.VMEM((SB_HEADS, tq, LANES), F32),
                        pltpu.VMEM((SB_HEADS // 2, tq, LANES), F32), pltpu.VMEM((SUBLANES, LANES), F32)],
        compiler_params=pltpu.CompilerParams(dimension_semantics=("arbitrary",) * 2,
                                             vmem_limit_bytes=VMEM_LIMIT_BYTES),
        name="sb_attention",
    )(q, k, v, nsuf)


def _mla_body(q_ref, k_ref, v_ref, o_ref, *, tq, n_q):
    i = pl.program_id(1)
    lane = lax.broadcasted_iota(jnp.int32, (1, LANES), 1)
    lo = lane < MLA_V_DIM
    visible = (lax.broadcasted_iota(jnp.int32, (tq, tq), 1) <= lax.broadcasted_iota(jnp.int32, (tq, tq), 0))

    def run(n_tiles):
        m = [None] * MLA_HEADS
        acc = [None] * MLA_HEADS

        def probs(it):
            hd, rows = it["hd"], it["rows"]
            q = q_ref[0, :, hd * MLA_HEAD_PAD:(hd + 1) * MLA_HEAD_PAD]
            kt = k_ref[0, rows, hd * MLA_HEAD_PAD:(hd + 1) * MLA_HEAD_PAD]
            s = _dot_nt(q, kt)
            if m[hd] is None:
                s = jnp.where(visible, s, -jnp.inf)
                m_new = jnp.broadcast_to(jnp.max(s, axis=-1, keepdims=True), (tq, LANES))
                it["alpha"] = None
            else:
                m_new = jnp.maximum(m[hd], jnp.max(s, axis=-1, keepdims=True))
                it["alpha"] = jnp.exp2(m[hd] - m_new)
            it["p"] = jnp.exp2(s - jnp.tile(m_new, (1, tq // LANES))).astype(BF16)
            m[hd] = m_new

        def values(it):
            hd = it["hd"]
            vt = v_ref[0, it["rows"], (hd // 2) * LANES:(hd // 2 + 1) * LANES]
            one = jnp.ones_like(vt)
            v_ext = jnp.where(lo, vt, one) if hd % 2 == 0 else jnp.where(lo, one, vt)
            pv = _dot(it["p"], v_ext)
            acc[hd] = pv if it["alpha"] is None else it["alpha"] * acc[hd] + pv

        order = [n_tiles - 1] + list(range(n_tiles - 1))
        items = [dict(hd=hd, rows=slice(j * tq, (j + 1) * tq)) for j in order for hd in range(MLA_HEADS)]
        _emit_skewed(items, (probs, values), skew=MLA_HEADS)
        for pair in range(MLA_HEADS // 2):
            a0, a1 = acc[2 * pair], acc[2 * pair + 1]
            o_ref[0, :, pair * LANES:(pair + 1) * LANES] = jnp.where(
                lo, a0 / pltpu.roll(a0, MLA_V_DIM, 1), a1 / pltpu.roll(a1, MLA_V_DIM, 1)).astype(o_ref.dtype)

    for c in range(n_q):
        pl.when(i == c)(functools.partial(run, c + 1))


def _mla_attention(q, k, v, *, tq):
    b, s, wq = q.shape
    w = v.shape[-1]
    return pl.pallas_call(
        functools.partial(_mla_body, tq=tq, n_q=s // tq),
        grid=(b, s // tq),
        in_specs=[pl.BlockSpec((1, tq, wq), lambda bi, i: (bi, i, 0)),
                  pl.BlockSpec((1, s, wq), lambda bi, i: (bi, 0, 0)),
                  pl.BlockSpec((1, s, w), lambda bi, i: (bi, 0, 0))],
        out_specs=pl.BlockSpec((1, tq, w), lambda bi, i: (bi, i, 0)),
        out_shape=jax.ShapeDtypeStruct((b, s, w), BF16),
        compiler_params=pltpu.CompilerParams(dimension_semantics=("arbitrary",) * 2,
                                             vmem_limit_bytes=VMEM_LIMIT_BYTES),
        name="mla_attention",
    )(q, k, v)


def _ffn_body(x_ref, osb_ref, omla_ref, g_sb_ref, g_mla_ref, w_out_ref, g_ffn_ref, w_up_ref, conv_ref,
              w_down_ref, g_final_ref, out_ref, h_ref, acc_ref, u_ref, *, tm, ck, tiles_per_seq):
    step = pl.program_id(0)
    halo = BF16_SUBLANES
    d = x_ref.shape[-1]

    @pl.when(step % tiles_per_seq == 0)
    def _():
        h_ref[0:halo, :] = jnp.zeros((halo, d), BF16)

    @pl.when(step % tiles_per_seq != 0)
    def _():
        h_ref[0:halo, :] = h_ref[tm:tm + halo, :]

    o = jnp.concatenate([_rms(osb_ref[...].astype(F32), g_sb_ref[...]),
                         _rms(omla_ref[...].astype(F32), g_mla_ref[...])], axis=-1)
    x2 = x_ref[...] + _dot(o.astype(BF16), w_out_ref[...])
    acc_ref[...] = x2
    h_ref[halo:, :] = _rms(x2, g_ffn_ref[...]).astype(BF16)

    def cols(ref, c):
        return ref[:, c * ck:(c + 1) * ck], ref[:, D_FF + c * ck:D_FF + (c + 1) * ck]

    def up(c, slot):
        w_gate, w_val = cols(w_up_ref, c)
        h = h_ref[...]
        u_ref[slot, :, :ck] = _dot(h, w_gate)
        u_ref[slot, :, ck:] = _dot(h, w_val)

    def down(c, slot):
        u = u_ref[slot]
        cw = jnp.concatenate(cols(conv_ref, c), axis=1)
        y = u[halo:] * cw[CONV_WIDTH - 1:CONV_WIDTH] + cw[CONV_WIDTH:CONV_WIDTH + 1]
        for back in range(1, CONV_WIDTH):
            tap = cw[CONV_WIDTH - 1 - back:CONV_WIDTH - back]
            y = y + pltpu.roll(u, back, 0)[halo:] * tap
        gate, val = y[:, :ck], y[:, ck:]
        act = gate * (1.0 / (1.0 + jnp.exp(-gate))) * val
        acc_ref[...] += _dot(act.astype(BF16), w_down_ref[c * ck:(c + 1) * ck, :])

    n_chunks = D_FF // ck
    for c0 in range(0, n_chunks, FFN_GROUP):
        size = min(FFN_GROUP, n_chunks - c0)
        for k in range(size):
            up(c0 + k, k)
        for k in range(size):
            down(c0 + k, k)
    out_ref[...] = _rms(acc_ref[...], g_final_ref[...])


def _ffn(x2d, osb, omla, g_sb, g_mla, w_out, g_ffn, w_up, conv_rows, w_down, g_final, *, tm, ck, seq):
    t, d = x2d.shape
    ck2 = 2 * ck
    row = lambda w: pl.BlockSpec((tm, w), lambda i: (i, 0))
    const = lambda a: pl.BlockSpec(a.shape, lambda i: (0,) * a.ndim, pipeline_mode=pl.Buffered(1))
    return pl.pallas_call(
        functools.partial(_ffn_body, tm=tm, ck=ck, tiles_per_seq=seq // tm),
        grid=(t // tm,),
        in_specs=[row(d), row(SB_WIDTH), row(MLA_WIDTH), const(g_sb), const(g_mla), const(w_out),
                  const(g_ffn), const(w_up), const(conv_rows), const(w_down), const(g_final)],
        out_specs=row(d),
        out_shape=jax.ShapeDtypeStruct((t, d), F32),
        scratch_shapes=[pltpu.VMEM((tm + BF16_SUBLANES, d), BF16), pltpu.VMEM((tm, d), F32),
                        pltpu.VMEM((FFN_GROUP, tm + BF16_SUBLANES, ck2), F32)],
        compiler_params=pltpu.CompilerParams(dimension_semantics=("arbitrary",),
                                             vmem_limit_bytes=VMEM_LIMIT_BYTES),
        name="out_proj_ffn",
    )(x2d, osb, omla, g_sb, g_mla, w_out, g_ffn, w_up, conv_rows, w_down, g_final)


def _tiles(seq):
    return dict(tm_in=1024, tq=256, tm_ffn=512, ck=256)


def kernel(x, positions, g_mix, w_in, g_cq, w_uq, g_ckv, w_ukv, g_sb_out, g_mla_out, w_out, g_ffn,
           w_up, conv_w, conv_b, w_down, g_final):
    b, s, d = x.shape
    depth = w_in.shape[0]
    t = b * s
    cfg = _tiles(s)
    assert s % cfg["tm_ffn"] == 0 and s % cfg["tq"] == 0 and t % cfg["tm_in"] == 0
    assert D_FF % cfg["ck"] == 0

    inv_freq = 1.0 / (ROPE_BASE ** (jnp.arange(ROPE_HALF, dtype=F32) * (2.0 / MLA_ROPE_DIM)))
    pad_l, pad_r = ROPE_LANE0, LANES - ROPE_LANE0 - MLA_ROPE_DIM
    invf = jnp.pad(jnp.tile(inv_freq, 2), (pad_l, pad_r))[None, :]
    sgn = jnp.pad(jnp.concatenate([-jnp.ones(ROPE_HALF, F32), jnp.ones(ROPE_HALF, F32)]), (pad_l, pad_r))[None, :]
    pos = positions.astype(F32).reshape(t, 1)

    x2d = x.reshape(t, d)
    for l in range(depth):
        n_main = 3 * SB_WIDTH + Q_LORA_RANK + KV_LORA_RANK
        w_in_p = jnp.concatenate(
            [w_in[l][:, :n_main], jnp.pad(w_in[l][:, n_main:], ((0, 0), (pad_l, pad_r)))], axis=1).astype(BF16)
        w_uq_p = jnp.pad(w_uq[l].reshape(Q_LORA_RANK, MLA_HEADS, MLA_QK_DIM),
                         ((0, 0), (0, 0), (0, MLA_HEAD_PAD - MLA_QK_DIM))
                         ).reshape(Q_LORA_RANK, MLA_HEADS * MLA_HEAD_PAD).astype(BF16)
        w_ukv_h = w_ukv[l].reshape(KV_LORA_RANK, MLA_HEADS, MLA_NOPE_DIM + MLA_V_DIM)
        w_uk_p = jnp.pad(w_ukv_h[:, :, :MLA_NOPE_DIM], ((0, 0), (0, 0), (0, MLA_HEAD_PAD - MLA_NOPE_DIM))
                         ).reshape(KV_LORA_RANK, MLA_HEADS * MLA_HEAD_PAD).astype(BF16)
        w_uv = w_ukv_h[:, :, MLA_NOPE_DIM:].reshape(KV_LORA_RANK, MLA_WIDTH).astype(BF16)

        qsb, ksb, vsb, qm, km, vm = _in_proj(
            x2d, pos, g_mix[l][None, :], w_in_p, g_cq[l][None, :], w_uq_p, g_ckv[l][None, :], w_uk_p, w_uv,
            invf, sgn, tm=cfg["tm_in"])

        o_sb = _sb_attention(qsb.reshape(b, s, -1), ksb.reshape(b, s, -1), vsb.reshape(b, s, -1), tq=cfg["tq"])
        o_mla = _mla_attention(qm.reshape(b, s, -1), km.reshape(b, s, -1), vm.reshape(b, s, -1), tq=cfg["tq"])

        conv_rows = jnp.concatenate([conv_w[l], conv_b[l][None, :]], axis=0)
        conv_rows = jnp.pad(conv_rows, ((0, 8 - conv_rows.shape[0]), (0, 0)))
        last = l == depth - 1
        g_last = g_final[None, :] if last else None
        assert last, "the fused final rmsnorm assumes a single layer"
        x2d = _ffn(x2d, o_sb.reshape(t, -1), o_mla.reshape(t, -1), g_sb_out[l][None, :], g_mla_out[l][None, :],
                   w_out[l].astype(BF16), g_ffn[l][None, :], w_up[l].astype(BF16), conv_rows,
                   w_down[l].astype(BF16), g_last, tm=cfg["tm_ffn"], ck=cfg["ck"], seq=s)
    return x2d.reshape(b, s, d)
```
